```python
import math
import jax, jax.numpy as jnp
from jax import lax
import numpy as np

D_MODEL = 2048
BATCH = 4
SEQ = 4096
DEPTH = 2

CHUNK = 64
N_EVEN = (DEPTH + 1) // 2
N_ODD = DEPTH // 2
EPS = 1e-6

CONV_DIM = D_MODEL
CONV_WIDTH = 31
GMLP_DIM = D_MODEL
GMLP_GROUPS = 8
GMLP_GROUP_DIM = GMLP_DIM // GMLP_GROUPS
GMLP_CHUNK = 128
EVEN_IN = 2 * CONV_DIM + 2 * GMLP_DIM
EVEN_OUT = CONV_DIM + GMLP_DIM
DIFF_HEADS = 8
DIFF_HEAD_DIM = D_MODEL // (2 * DIFF_HEADS)
DIFF_V_DIM = 2 * DIFF_HEAD_DIM
Q_BLOCK = 128
ODD_IN = 3 * D_MODEL
FFN_HIDDEN = ((8 * D_MODEL + 3 * 256 - 1) // (3 * 256)) * 256

kernel_name = "hybrid_conv_gmlp_diffattn_encoder"


def rms_norm(x, g):
    xf = x.astype(jnp.float32)
    y = xf * lax.rsqrt(jnp.mean(xf * xf, axis=-1, keepdims=True) + EPS)
    return (y * g.astype(jnp.float32)).astype(x.dtype)


def layer_norm(x, g, b):
    xf = x.astype(jnp.float32)
    mu = jnp.mean(xf, axis=-1, keepdims=True)
    var = jnp.mean(jnp.square(xf - mu), axis=-1, keepdims=True)
    y = (xf - mu) * lax.rsqrt(var + EPS) * g.astype(jnp.float32) + b.astype(jnp.float32)
    return y.astype(x.dtype)


def conformer_conv(a_val, a_gate, w_dw, b_dw, ln_g, ln_b):
    h = a_val * jax.nn.sigmoid(a_gate)
    h = lax.conv_general_dilated(
        h, w_dw[:, None, :], window_strides=(1,),
        padding=[(CONV_WIDTH - 1, 0)],
        dimension_numbers=("NWC", "WIO", "NWC"),
        feature_group_count=CONV_DIM) + b_dw
    h = layer_norm(h, ln_g, ln_b)
    return jax.nn.silu(h)


def chunked_spatial_gating(u, v, w_s, b_s, ln_g, ln_b):
    B, S, _ = u.shape
    v = layer_norm(v, ln_g, ln_b)
    v = v.reshape(B, S // GMLP_CHUNK, GMLP_CHUNK, GMLP_GROUPS, GMLP_GROUP_DIM)
    pos = jnp.arange(GMLP_CHUNK)
    mask = (pos[None, :] // CHUNK) <= (pos[:, None] // CHUNK)
    w = jnp.where(mask[None], w_s, jnp.zeros_like(w_s))
    s = jnp.einsum("gij,bcjgd->bcigd", w, v) + b_s.T[None, None, :, :, None]
    return u * s.reshape(B, S, GMLP_DIM)


def diff_attention(h, w_qkv, w_o, lq1, lk1, lq2, lk2, subln_g, lambda_init):
    B, S, _ = h.shape
    qkv = h @ w_qkv
    q, k, v = jnp.split(qkv, 3, axis=-1)
    q = q.reshape(B, S, DIFF_HEADS, 2, DIFF_HEAD_DIM)
    k = k.reshape(B, S, DIFF_HEADS, 2, DIFF_HEAD_DIM)
    v = v.reshape(B, S, DIFF_HEADS, DIFF_V_DIM)
    f32 = jnp.float32
    lam = (jnp.exp(jnp.sum(lq1.astype(f32) * lk1.astype(f32)))
           - jnp.exp(jnp.sum(lq2.astype(f32) * lk2.astype(f32))) + lambda_init)
    scale = DIFF_HEAD_DIM ** -0.5
    nb = S // Q_BLOCK
    q_blocks = q.reshape(B, nb, Q_BLOCK, DIFF_HEADS, 2, DIFF_HEAD_DIM).transpose(1, 0, 2, 3, 4, 5)
    key_chunk = jnp.arange(S) // CHUNK

    def block(args):
        q_blk, idx = args
        scores = jnp.einsum("bqhcd,bkhcd->bhcqk", q_blk, k).astype(f32) * scale
        q_chunk = (idx * Q_BLOCK + jnp.arange(Q_BLOCK)) // CHUNK
        mask = key_chunk[None, :] <= q_chunk[:, None]
        scores = jnp.where(mask, scores, -jnp.inf)
        p = jax.nn.softmax(scores, axis=-1)
        attn = p[:, :, 0] - lam * p[:, :, 1]
        return jnp.einsum("bhqk,bkhe->bqhe", attn.astype(v.dtype), v)

    o = lax.map(block, (q_blocks, jnp.arange(nb)))
    o = o.transpose(1, 0, 2, 3, 4).reshape(B, S, DIFF_HEADS, DIFF_V_DIM)
    o = rms_norm(o, subln_g) * (1.0 - lambda_init)
    return o.reshape(B, S, D_MODEL) @ w_o


def swiglu(h, w_gate_up, w_down):
    g, u = jnp.split(h @ w_gate_up, 2, axis=-1)
    return (jax.nn.silu(g) * u) @ w_down


def setup_inputs(seed: int = 0) -> dict:
    key = jax.random.key(seed)
    ks = jax.random.split(key, 24)
    f32 = jnp.float32
    nrm = lambda k, shape, s: (jax.random.normal(k, shape, f32) * s)
    gain = lambda k, shape: 1.0 + 0.05 * jax.random.normal(k, shape, f32)
    return {
        "x": jax.random.normal(ks[0], (BATCH, SEQ, D_MODEL), f32),
        "w_in_even": nrm(ks[1], (N_EVEN, D_MODEL, EVEN_IN), D_MODEL ** -0.5),
        "conv_w": nrm(ks[2], (N_EVEN, CONV_WIDTH, CONV_DIM), CONV_WIDTH ** -0.5),
        "conv_b": nrm(ks[3], (N_EVEN, CONV_DIM), 0.01),
        "conv_ln_g": gain(ks[4], (N_EVEN, CONV_DIM)),
        "conv_ln_b": nrm(ks[5], (N_EVEN, CONV_DIM), 0.01),
        "gmlp_ln_g": gain(ks[6], (N_EVEN, GMLP_DIM)),
        "gmlp_ln_b": nrm(ks[7], (N_EVEN, GMLP_DIM), 0.01),
        "gmlp_w_s": nrm(ks[8], (N_EVEN, GMLP_GROUPS, GMLP_CHUNK, GMLP_CHUNK), GMLP_CHUNK ** -0.5),
        "gmlp_b_s": 1.0 + nrm(ks[9], (N_EVEN, GMLP_GROUPS, GMLP_CHUNK), 0.01),
        "w_out_even": nrm(ks[10], (N_EVEN, EVEN_OUT, D_MODEL), EVEN_OUT ** -0.5),
        "w_qkv_odd": nrm(ks[11], (N_ODD, D_MODEL, ODD_IN), D_MODEL ** -0.5),
        "w_o_odd": nrm(ks[12], (N_ODD, D_MODEL, D_MODEL), D_MODEL ** -0.5),
        "lambda_q1": nrm(ks[13], (N_ODD, DIFF_HEAD_DIM), 0.1),
        "lambda_k1": nrm(ks[14], (N_ODD, DIFF_HEAD_DIM), 0.1),
        "lambda_q2": nrm(ks[15], (N_ODD, DIFF_HEAD_DIM), 0.1),
        "lambda_k2": nrm(ks[16], (N_ODD, DIFF_HEAD_DIM), 0.1),
        "subln_g": gain(ks[17], (N_ODD, DIFF_V_DIM)),
        "mix_norm_g": gain(ks[18], (DEPTH, D_MODEL)),
        "ffn_norm_g": gain(ks[19], (DEPTH, D_MODEL)),
        "w_gate_up": nrm(ks[20], (DEPTH, D_MODEL, 2 * FFN_HIDDEN), D_MODEL ** -0.5),
        "w_down": nrm(ks[21], (DEPTH, FFN_HIDDEN, D_MODEL), FFN_HIDDEN ** -0.5),
        "final_norm_g": gain(ks[22], (D_MODEL,)),
    }


def reference(x, w_in_even, conv_w, conv_b, conv_ln_g, conv_ln_b, gmlp_ln_g, gmlp_ln_b,
              gmlp_w_s, gmlp_b_s, w_out_even, w_qkv_odd, w_o_odd, lambda_q1, lambda_k1,
              lambda_q2, lambda_k2, subln_g, mix_norm_g, ffn_norm_g, w_gate_up, w_down,
              final_norm_g):
    for layer in range(DEPTH):
        h = rms_norm(x, mix_norm_g[layer])
        if layer % 2 == 0:
            i = layer // 2
            z = h @ w_in_even[i]
            a_val, a_gate, b_u, b_v = jnp.split(
                z, [CONV_DIM, 2 * CONV_DIM, 2 * CONV_DIM + GMLP_DIM], axis=-1)
            a = conformer_conv(a_val, a_gate, conv_w[i], conv_b[i], conv_ln_g[i], conv_ln_b[i])
            b = chunked_spatial_gating(jax.nn.gelu(b_u, approximate=False),
                                       jax.nn.gelu(b_v, approximate=False),
                                       gmlp_w_s[i], gmlp_b_s[i], gmlp_ln_g[i], gmlp_ln_b[i])
            y = jnp.concatenate([a, b], axis=-1) @ w_out_even[i]
        else:
            i = layer // 2
            lambda_init = 0.8 - 0.6 * math.exp(-0.3 * layer)
            y = diff_attention(h, w_qkv_odd[i], w_o_odd[i], lambda_q1[i], lambda_k1[i],
                               lambda_q2[i], lambda_k2[i], subln_g[i], lambda_init)
        x = x + y
        x = x + swiglu(rms_norm(x, ffn_norm_g[layer]), w_gate_up[layer], w_down[layer])
    return rms_norm(x, final_norm_g)
```

```python
import functools
import math

import jax
import jax.numpy as jnp
from jax import lax
from jax.experimental import pallas as pl
from jax.experimental.pallas import tpu as pltpu

F32 = jnp.float32
BF16 = jnp.bfloat16

EPS = 1e-6
CHUNK = 64
CONV_WIDTH = 31
GMLP_GROUPS = 8
GMLP_CHUNK = 128
DIFF_HEADS = 8
LOG2E = 1.4426950408889634

V7X_LANES = 128
V7X_SUBLANES = 8
V7X_VMEM_LIMIT_BYTES = 56 * 1024 * 1024

ROW_TILE = 1024
CONV_HALO = 32


def _dot(a, b):
    return jnp.dot(a, b, preferred_element_type=F32)


def _rms_rows(x, g):
    ms = jnp.mean(x * x, axis=-1, keepdims=True)
    return x * lax.rsqrt(ms + EPS) * g


def _layer_norm_rows(x, g, b):
    mu = jnp.mean(x, axis=-1, keepdims=True)
    d = x - mu
    var = jnp.mean(d * d, axis=-1, keepdims=True)
    return d * lax.rsqrt(var + EPS) * g + b


def _silu(x):
    return x * jax.nn.sigmoid(x)


def _gelu_exact(x):
    return 0.5 * x * (1.0 + lax.erf(x * (1.0 / math.sqrt(2.0))))


def _norm_rows_into(x_ref, g_ref, h_ref, rows_per_step=128):
    n = x_ref.shape[0] // rows_per_step

    def body(r, carry):
        rs = pl.ds(pl.multiple_of(r * rows_per_step, rows_per_step), rows_per_step)
        h_ref[rs, :] = _rms_rows(x_ref[rs, :], g_ref[...]).astype(h_ref.dtype)
        return carry

    lax.fori_loop(0, n, body, 0)


def _params(vmem_bytes, semantics):
    return pltpu.CompilerParams(
        dimension_semantics=semantics,
        vmem_limit_bytes=min(int(vmem_bytes), V7X_VMEM_LIMIT_BYTES),
    )


def _norm_matmul_kernel(x_ref, g_ref, w_ref, cs_ref, o_ref, h_ref):
    @pl.when(pl.program_id(1) == 0)
    def _():
        _norm_rows_into(x_ref, g_ref, h_ref)

    o_ref[...] = (_dot(h_ref[...], w_ref[...]) * cs_ref[...]).astype(o_ref.dtype)


def _norm_matmul(x, g, w, colscale, tn=1024):
    m, d = x.shape
    n = w.shape[1]
    tm = ROW_TILE
    vmem = tm * d * 4 + tm * d * 2 + 2 * d * tn * 2 + 2 * tm * tn * 2 + tm * tn * 4 + (4 << 20)
    return pl.pallas_call(
        _norm_matmul_kernel,
        grid=(m // tm, n // tn),
        in_specs=[
            pl.BlockSpec((tm, d), lambda i, j: (i, 0), pipeline_mode=pl.Buffered(1)),
            pl.BlockSpec((1, d), lambda i, j: (0, 0)),
            pl.BlockSpec((d, tn), lambda i, j: (0, j)),
            pl.BlockSpec((1, tn), lambda i, j: (0, j)),
        ],
        out_specs=pl.BlockSpec((tm, tn), lambda i, j: (i, j)),
        out_shape=jax.ShapeDtypeStruct((m, n), BF16),
        scratch_shapes=[pltpu.VMEM((tm, d), BF16)],
        compiler_params=_params(vmem, ("parallel", "arbitrary")),
        name="qkv_proj",
    )(x, g.reshape(1, d), w, colscale)


def _in_proj_kernel(x_ref, g_ref, wa_ref, wg_ref, wu_ref, wv_ref, glu_ref, gu_ref, gv_ref, h_ref):
    @pl.when(pl.program_id(1) == 0)
    def _():
        _norm_rows_into(x_ref, g_ref, h_ref)

    h = h_ref[...]
    glu_ref[...] = (_dot(h, wa_ref[...]) * jax.nn.sigmoid(_dot(h, wg_ref[...]))).astype(glu_ref.dtype)
    gu_ref[...] = _gelu_exact(_dot(h, wu_ref[...])).astype(gu_ref.dtype)
    gv_ref[...] = _gelu_exact(_dot(h, wv_ref[...])).astype(gv_ref.dtype)


def _in_proj(x, g, w, tn=512):
    m, d = x.shape
    sec = w.shape[1] // 4
    nb = sec // tn
    tm = ROW_TILE
    vmem = tm * d * 4 + tm * d * 2 + 4 * 2 * d * tn * 2 + 3 * 2 * tm * tn * 2 + 4 * tm * tn * 4 + (4 << 20)
    wspec = lambda s: pl.BlockSpec((d, tn), lambda i, j: (0, s * nb + j))
    ospec = pl.BlockSpec((tm, tn), lambda i, j: (i, j))
    oshape = jax.ShapeDtypeStruct((m, sec), BF16)
    return pl.pallas_call(
        _in_proj_kernel,
        grid=(m // tm, nb),
        in_specs=[
            pl.BlockSpec((tm, d), lambda i, j: (i, 0), pipeline_mode=pl.Buffered(1)),
            pl.BlockSpec((1, d), lambda i, j: (0, 0)),
            wspec(0), wspec(1), wspec(2), wspec(3),
        ],
        out_specs=[ospec, ospec, ospec],
        out_shape=[oshape, oshape, oshape],
        scratch_shapes=[pltpu.VMEM((tm, d), BF16)],
        compiler_params=_params(vmem, ("parallel", "arbitrary")),
        name="even_in_proj",
    )(x, g.reshape(1, d), w, w, w, w)


def _conv_kernel(main_ref, halo_ref, w_ref, b_ref, lg_ref, lb_ref, o_ref, win_ref, sh_ref, c_ref, *,
                 tiles_per_seq):
    tc, ch = main_ref.shape
    first = (pl.program_id(0) % tiles_per_seq) == 0
    halo = halo_ref[...].astype(F32)
    win_ref[0:CONV_HALO, :] = jnp.where(first, 0.0, halo)
    win_ref[CONV_HALO:, :] = main_ref[...].astype(F32)

    base = CONV_HALO - (CONV_WIDTH - 1)
    sh_rows = sh_ref.shape[1]
    rb = 64
    for cb in range(ch // V7X_LANES):
        cs = slice(cb * V7X_LANES, (cb + 1) * V7X_LANES)
        for s in range(1, V7X_SUBLANES):
            sh_ref[s - 1] = win_ref[s:s + sh_rows, cs]

        def rows(r, carry):
            r0 = pl.multiple_of(r * rb, rb)
            acc = jnp.zeros((rb, V7X_LANES), F32) + b_ref[:, cs]
            for k in range(CONV_WIDTH):
                s = (base + k) % V7X_SUBLANES
                rs = pl.ds(r0 + (base + k - s), rb)
                tap = win_ref[rs, cs] if s == 0 else sh_ref[s - 1, rs, :]
                acc = acc + tap * w_ref[k:k + 1, cs]
            c_ref[pl.ds(r0, rb), cs] = acc
            return carry

        lax.fori_loop(0, tc // rb, rows, 0)

    nr = 32

    def norm(r, carry):
        rs = pl.ds(pl.multiple_of(r * nr, nr), nr)
        y = _layer_norm_rows(c_ref[rs, :], lg_ref[...], lb_ref[...])
        o_ref[rs, :] = _silu(y).astype(o_ref.dtype)
        return carry

    lax.fori_loop(0, tc // nr, norm, 0)


def _conv_ln_silu(glu, seq, w, b, lg, lb, tc=512):
    m, ch = glu.shape
    hb = tc // CONV_HALO
    row = lambda v: v.reshape(1, ch)
    vmem = 2 * tc * ch * 2 * 2 + (tc + CONV_HALO) * ch * 4 + tc * ch * 4 + (4 << 20)
    return pl.pallas_call(
        functools.partial(_conv_kernel, tiles_per_seq=seq // tc),
        grid=(m // tc,),
        in_specs=[
            pl.BlockSpec((tc, ch), lambda i: (i, 0)),
            pl.BlockSpec((CONV_HALO, ch), lambda i: (jnp.maximum(i * hb - 1, 0), 0)),
            pl.BlockSpec((CONV_WIDTH, ch), lambda i: (0, 0)),
            pl.BlockSpec((1, ch), lambda i: (0, 0)),
            pl.BlockSpec((1, ch), lambda i: (0, 0)),
            pl.BlockSpec((1, ch), lambda i: (0, 0)),
        ],
        out_specs=pl.BlockSpec((tc, ch), lambda i: (i, 0)),
        out_shape=jax.ShapeDtypeStruct((m, ch), BF16),
        scratch_shapes=[
            pltpu.VMEM((tc + CONV_HALO, ch), F32),
            pltpu.VMEM((V7X_SUBLANES - 1, tc + CONV_HALO - V7X_SUBLANES, V7X_LANES), F32),
            pltpu.VMEM((tc, ch), F32),
        ],
        compiler_params=_params(vmem, ("parallel",)),
        name="conv_ln_swish",
    )(glu, glu, w, row(b), row(lg), row(lb))


def _gating_kernel(u_ref, v_ref, lg_ref, lb_ref, ws_ref, bs_ref, o_ref, vn_ref):
    tg, ch = u_ref.shape
    gd = ch // GMLP_GROUPS
    nr = 32

    def norm(r, carry):
        rs = pl.ds(pl.multiple_of(r * nr, nr), nr)
        vn_ref[rs, :] = _layer_norm_rows(v_ref[rs, :].astype(F32), lg_ref[...], lb_ref[...]).astype(vn_ref.dtype)
        return carry

    lax.fori_loop(0, tg // nr, norm, 0)

    ii = lax.broadcasted_iota(jnp.int32, (GMLP_CHUNK, GMLP_CHUNK), 0) // CHUNK
    jj = lax.broadcasted_iota(jnp.int32, (GMLP_CHUNK, GMLP_CHUNK), 1) // CHUNK
    keep = jj <= ii
    for g in range(GMLP_GROUPS):
        wg = jnp.where(keep, ws_ref[g], 0.0).astype(BF16)
        bcol = bs_ref[:, g:g + 1]
        cs = slice(g * gd, (g + 1) * gd)
        for c in range(tg // GMLP_CHUNK):
            rs = slice(c * GMLP_CHUNK, (c + 1) * GMLP_CHUNK)
            s = _dot(wg, vn_ref[rs, cs]) + bcol
            o_ref[rs, cs] = (u_ref[rs, cs].astype(F32) * s).astype(o_ref.dtype)


def _gating(gu, gv, lg, lb, w_s, b_s, tg=512):
    m, ch = gu.shape
    row = lambda v: v.reshape(1, ch)
    vmem = 3 * 2 * tg * ch * 2 + tg * ch * 2 + (6 << 20)
    blk = pl.BlockSpec((tg, ch), lambda i: (i, 0))
    return pl.pallas_call(
        _gating_kernel,
        grid=(m // tg,),
        in_specs=[
            blk, blk,
            pl.BlockSpec((1, ch), lambda i: (0, 0)),
            pl.BlockSpec((1, ch), lambda i: (0, 0)),
            pl.BlockSpec(w_s.shape, lambda i: (0, 0, 0)),
            pl.BlockSpec((GMLP_CHUNK, GMLP_GROUPS), lambda i: (0, 0)),
        ],
        out_specs=blk,
        out_shape=jax.ShapeDtypeStruct((m, ch), BF16),
        scratch_shapes=[pltpu.VMEM((tg, ch), BF16)],
        compiler_params=_params(vmem, ("parallel",)),
        name="spatial_gating",
    )(gu, gv, row(lg), row(lb), w_s, b_s.T)


def _proj_res_kernel(*refs, n_parts):
    a_refs = refs[:n_parts]
    w_refs = refs[n_parts:2 * n_parts]
    res_ref, o_ref = refs[2 * n_parts], refs[2 * n_parts + 1]
    acc = res_ref[...]
    for a_ref, w_ref in zip(a_refs, w_refs):
        acc = acc + _dot(a_ref[...], w_ref[...])
    o_ref[...] = acc


def _proj_res(parts, w, res, tn=512):
    m, n = res.shape
    kp = parts[0].shape[1]
    n_parts = len(parts)
    tm = ROW_TILE
    vmem = n_parts * 2 * (tm * kp * 2 + kp * tn * 2) + 4 * tm * tn * 4 + 2 * tm * tn * 4 + (4 << 20)
    a_specs = [pl.BlockSpec((tm, kp), lambda i, j: (i, 0)) for _ in parts]
    w_specs = [pl.BlockSpec((kp, tn), lambda i, j, p=p: (p, j)) for p in range(n_parts)]
    return pl.pallas_call(
        functools.partial(_proj_res_kernel, n_parts=n_parts),
        grid=(m // tm, n // tn),
        in_specs=a_specs + w_specs + [pl.BlockSpec((tm, tn), lambda i, j: (i, j))],
        out_specs=pl.BlockSpec((tm, tn), lambda i, j: (i, j)),
        out_shape=jax.ShapeDtypeStruct((m, n), F32),
        compiler_params=_params(vmem, ("parallel", "arbitrary")),
        name="proj_residual",
    )(*parts, *([w] * n_parts), res)


def _ffn_kernel(x_ref, g_ref, wg_ref, wu_ref, wd_ref, fg_ref, o_ref, h_ref, *, final_norm, row_chunk):
    f = pl.program_id(1)

    @pl.when(f == 0)
    def _():
        _norm_rows_into(x_ref, g_ref, h_ref)
        o_ref[...] = x_ref[...]

    for r in range(x_ref.shape[0] // row_chunk):
        rs = slice(r * row_chunk, (r + 1) * row_chunk)
        h = h_ref[rs, :]
        gate = _dot(h, wg_ref[...])
        up = _dot(h, wu_ref[...])
        act = (_silu(gate) * up).astype(BF16)
        o_ref[rs, :] += _dot(act, wd_ref[...])

    if final_norm:
        @pl.when(f == pl.num_programs(1) - 1)
        def _():
            nr = 128

            def body(r, carry):
                rs = pl.ds(pl.multiple_of(r * nr, nr), nr)
                o_ref[rs, :] = _rms_rows(o_ref[rs, :], fg_ref[...])
                return carry

            lax.fori_loop(0, o_ref.shape[0] // nr, body, 0)


def _ffn(x, g, w_gate_up, w_down, final_g=None, tf=512, row_chunk=256):
    m, d = x.shape
    hidden = w_down.shape[0]
    nf = hidden // tf
    tm = ROW_TILE
    final_norm = final_g is not None
    fg = (final_g if final_norm else g).reshape(1, d)
    vmem = (tm * d * 4 + tm * d * 2 + 2 * tm * d * 4 + 3 * 2 * d * tf * 2
            + 2 * row_chunk * tf * 4 + row_chunk * d * 4 + (6 << 20))
    return pl.pallas_call(
        functools.partial(_ffn_kernel, final_norm=final_norm, row_chunk=row_chunk),
        grid=(m // tm, nf),
        in_specs=[
            pl.BlockSpec((tm, d), lambda i, f: (i, 0), pipeline_mode=pl.Buffered(1)),
            pl.BlockSpec((1, d), lambda i, f: (0, 0)),
            pl.BlockSpec((d, tf), lambda i, f: (0, f)),
            pl.BlockSpec((d, tf), lambda i, f: (0, nf + f)),
            pl.BlockSpec((tf, d), lambda i, f: (f, 0)),
            pl.BlockSpec((1, d), lambda i, f: (0, 0)),
        ],
        out_specs=pl.BlockSpec((tm, d), lambda i, f: (i, 0)),
        out_shape=jax.ShapeDtypeStruct((m, d), F32),
        scratch_shapes=[pltpu.VMEM((tm, d), BF16)],
        compiler_params=_params(vmem, ("parallel", "arbitrary")),
        name="swiglu_ffn",
    )(x, g.reshape(1, d), w_gate_up, w_gate_up, w_down, fg)


def _diff_attn_kernel(q_ref, k_ref, v_ref, lq1_ref, lk1_ref, lq2_ref, lk2_ref, sg_ref, o_ref,
                      acc_ref, m_ref, l_ref, *, lambda_init, tk):
    tq, hd2 = q_ref.shape
    hd = hd2 // 2
    qi = pl.program_id(2)
    q = q_ref[...]

    m_ref[...] = jnp.full(m_ref.shape, -jnp.inf, F32)
    l_ref[...] = jnp.zeros(l_ref.shape, F32)
    acc_ref[...] = jnp.zeros(acc_ref.shape, F32)

    def step(kb, masked):
        ks = pl.ds(pl.multiple_of(kb * tk, tk), tk)
        kblk = k_ref[ks, :]
        vblk = v_ref[ks, :]
        if masked:
            qc_id = lax.broadcasted_iota(jnp.int32, (tq, tk), 0) // CHUNK
            kc_id = lax.broadcasted_iota(jnp.int32, (tq, tk), 1) // CHUNK
            keep = kc_id <= qc_id
        for c in range(2):
            s = lax.dot_general(q[:, c * hd:(c + 1) * hd], kblk[:, c * hd:(c + 1) * hd],
                                (((1,), (1,)), ((), ())), preferred_element_type=F32)
            if masked:
                s = jnp.where(keep, s, -jnp.inf)
            m_prev = m_ref[c]
            m_new = jnp.maximum(m_prev, jnp.max(s, axis=-1, keepdims=True))
            alpha = jnp.exp2(m_prev - m_new)
            p = jnp.exp2(s - m_new)
            l_ref[c] = alpha * l_ref[c] + jnp.sum(p, axis=-1, keepdims=True)
            acc_ref[c] = alpha * acc_ref[c] + _dot(p.astype(BF16), vblk)
            m_ref[c] = m_new

    def full_block(kb, carry):
        step(kb, False)
        return carry

    lax.fori_loop(0, qi, full_block, 0)
    step(qi, True)

    lam = (jnp.exp(jnp.sum(lq1_ref[...] * lk1_ref[...], axis=-1, keepdims=True))
           - jnp.exp(jnp.sum(lq2_ref[...] * lk2_ref[...], axis=-1, keepdims=True)) + lambda_init)
    o = acc_ref[0] / l_ref[0] - lam * (acc_ref[1] / l_ref[1])
    o = _rms_rows(o, sg_ref[...]) * (1.0 - lambda_init)
    o_ref[...] = o.astype(o_ref.dtype)


def _diff_attn(qkv, batch, seq, lq1, lk1, lq2, lk2, subln_g, lambda_init, tq=512):
    m = qkv.shape[0]
    hw = qkv.shape[1] // (3 * DIFF_HEADS)
    nq = seq // tq
    row = lambda v: v.reshape(1, -1)
    vmem = 2 * tq * hw * 2 + 2 * 2 * seq * hw * 2 + 2 * tq * hw * 2 + 2 * tq * hw * 4 + 8 * tq * tq * 4 + (4 << 20)
    lspec = pl.BlockSpec((1, hw // 2), lambda b, h, i: (0, 0))
    return pl.pallas_call(
        functools.partial(_diff_attn_kernel, lambda_init=lambda_init, tk=tq),
        grid=(batch, DIFF_HEADS, nq),
        in_specs=[
            pl.BlockSpec((tq, hw), lambda b, h, i: (b * nq + i, h)),
            pl.BlockSpec((seq, hw), lambda b, h, i: (b, DIFF_HEADS + h)),
            pl.BlockSpec((seq, hw), lambda b, h, i: (b, 2 * DIFF_HEADS + h)),
            lspec, lspec, lspec, lspec,
            pl.BlockSpec((1, hw), lambda b, h, i: (0, 0)),
        ],
        out_specs=pl.BlockSpec((tq, hw), lambda b, h, i: (b * nq + i, h)),
        out_shape=jax.ShapeDtypeStruct((m, DIFF_HEADS * hw), BF16),
        scratch_shapes=[
            pltpu.VMEM((2, tq, hw), F32),
            pltpu.VMEM((2, tq, 1), F32),
            pltpu.VMEM((2, tq, 1), F32),
        ],
        compiler_params=_params(vmem, ("parallel", "parallel", "arbitrary")),
        name="diff_attention",
    )(qkv, qkv, qkv, row(lq1), row(lk1), row(lq2), row(lk2), row(subln_g))


def kernel(x, w_in_even, conv_w, conv_b, conv_ln_g, conv_ln_b, gmlp_ln_g, gmlp_ln_b, gmlp_w_s, gmlp_b_s,
           w_out_even, w_qkv_odd, w_o_odd, lambda_q1, lambda_k1, lambda_q2, lambda_k2, subln_g, mix_norm_g,
           ffn_norm_g, w_gate_up, w_down, final_norm_g):
    batch, seq, d = x.shape
    depth = mix_norm_g.shape[0]
    xf = x.reshape(batch * seq, d)
    head_dim = d // (2 * DIFF_HEADS)
    for layer in range(depth):
        i = layer // 2
        if layer % 2 == 0:
            glu, gu, gv = _in_proj(xf, mix_norm_g[layer], w_in_even[i].astype(BF16))
            a = _conv_ln_silu(glu, seq, conv_w[i], conv_b[i], conv_ln_g[i], conv_ln_b[i])
            b = _gating(gu, gv, gmlp_ln_g[i], gmlp_ln_b[i], gmlp_w_s[i], gmlp_b_s[i])
            xf = _proj_res([a, b], w_out_even[i].astype(BF16), xf)
        else:
            lambda_init = 0.8 - 0.6 * math.exp(-0.3 * layer)
            colscale = jnp.concatenate([
                jnp.full((1, d), head_dim ** -0.5 * LOG2E, F32), jnp.ones((1, 2 * d), F32)], axis=1)
            qkv = _norm_matmul(xf, mix_norm_g[layer], w_qkv_odd[i].astype(BF16), colscale)
            o = _diff_attn(qkv, batch, seq, lambda_q1[i], lambda_k1[i], lambda_q2[i], lambda_k2[i],
                           subln_g[i], lambda_init)
            xf = _proj_res([o], w_o_odd[i].astype(BF16), xf)
        xf = _ffn(xf, ffn_norm_g[layer], w_gate_up[layer].astype(BF16), w_down[layer].astype(BF16),
                  final_g=final_norm_g if layer == depth - 1 else None)
    return xf.reshape(batch, seq, d)
```

```python
import functools
import math

import jax
import jax.numpy as jnp
from jax import lax
from jax.experimental import pallas as pl
from jax.experimental.pallas import tpu as pltpu

F32 = jnp.float32
BF16 = jnp.bfloat16

EPS = 1e-6
CHUNK = 64
CONV_WIDTH = 31
GMLP_GROUPS = 8
GMLP_CHUNK = 128
DIFF_HEADS = 8
LOG2E = 1.4426950408889634

V7X_LANES = 128
V7X_SUBLANES = 8
V7X_VMEM_LIMIT_BYTES = 56 * 1024 * 1024

ROW_TILE = 1024
CONV_HALO = 32


def _dot(a, b):
    return jnp.dot(a, b, preferred_element_type=F32)


def _rms_rows(x, g):
    ms = jnp.mean(x * x, axis=-1, keepdims=True)
    return x * lax.rsqrt(ms + EPS) * g


def _layer_norm_rows(x, g, b):
    mu = jnp.mean(x, axis=-1, keepdims=True)
    d = x - mu
    var = jnp.mean(d * d, axis=-1, keepdims=True)
    return d * lax.rsqrt(var + EPS) * g + b


def _silu(x):
    return x * jax.nn.sigmoid(x)


def _gelu_exact(x):
    return 0.5 * x * (1.0 + lax.erf(x * (1.0 / math.sqrt(2.0))))


def _norm_rows_into(x_ref, g_ref, h_ref, rows_per_step=128):
    n = x_ref.shape[0] // rows_per_step

    def body(r, carry):
        rs = pl.ds(pl.multiple_of(r * rows_per_step, rows_per_step), rows_per_step)
        h_ref[rs, :] = _rms_rows(x_ref[rs, :], g_ref[...]).astype(h_ref.dtype)
        return carry

    lax.fori_loop(0, n, body, 0)


def _params(vmem_bytes, semantics):
    return pltpu.CompilerParams(
        dimension_semantics=semantics,
        vmem_limit_bytes=min(int(vmem_bytes), V7X_VMEM_LIMIT_BYTES),
    )


def _norm_matmul_kernel(x_ref, g_ref, w_ref, cs_ref, o_ref, h_ref):
    @pl.when(pl.program_id(1) == 0)
    def _():
        _norm_rows_into(x_ref, g_ref, h_ref)

    o_ref[...] = (_dot(h_ref[...], w_ref[...]) * cs_ref[...]).astype(o_ref.dtype)


def _norm_matmul(x, g, w, colscale, tn=1024):
    m, d = x.shape
    n = w.shape[1]
    tm = ROW_TILE
    vmem = 2 * tm * d * 4 + tm * d * 2 + 2 * d * tn * 2 + 2 * tm * tn * 2 + tm * tn * 4 + (4 << 20)
    return pl.pallas_call(
        _norm_matmul_kernel,
        grid=(m // tm, n // tn),
        in_specs=[
            pl.BlockSpec((tm, d), lambda i, j: (i, 0)),
            pl.BlockSpec((1, d), lambda i, j: (0, 0)),
            pl.BlockSpec((d, tn), lambda i, j: (0, j)),
            pl.BlockSpec((1, tn), lambda i, j: (0, j)),
        ],
        out_specs=pl.BlockSpec((tm, tn), lambda i, j: (i, j)),
        out_shape=jax.ShapeDtypeStruct((m, n), BF16),
        scratch_shapes=[pltpu.VMEM((tm, d), BF16)],
        compiler_params=_params(vmem, ("parallel", "arbitrary")),
        name="qkv_proj",
    )(x, g.reshape(1, d), w, colscale)


def _in_proj_kernel(x_ref, g_ref, wa_ref, wg_ref, wu_ref, wv_ref, glu_ref, gu_ref, gv_ref, h_ref):
    @pl.when(pl.program_id(1) == 0)
    def _():
        _norm_rows_into(x_ref, g_ref, h_ref)

    h = h_ref[...]
    glu_ref[...] = (_dot(h, wa_ref[...]) * jax.nn.sigmoid(_dot(h, wg_ref[...]))).astype(glu_ref.dtype)
    gu_ref[...] = _gelu_exact(_dot(h, wu_ref[...])).astype(gu_ref.dtype)
    gv_ref[...] = _gelu_exact(_dot(h, wv_ref[...])).astype(gv_ref.dtype)


def _in_proj(x, g, w, tn=512):
    m, d = x.shape
    sec = w.shape[1] // 4
    nb = sec // tn
    tm = ROW_TILE
    vmem = 2 * tm * d * 4 + tm * d * 2 + 4 * 2 * d * tn * 2 + 3 * 2 * tm * tn * 2 + 4 * tm * tn * 4 + (4 << 20)
    wspec = lambda s: pl.BlockSpec((d, tn), lambda i, j: (0, s * nb + j))
    ospec = pl.BlockSpec((tm, tn), lambda i, j: (i, j))
    oshape = jax.ShapeDtypeStruct((m, sec), BF16)
    return pl.pallas_call(
        _in_proj_kernel,
        grid=(m // tm, nb),
        in_specs=[
            pl.BlockSpec((tm, d), lambda i, j: (i, 0)),
            pl.BlockSpec((1, d), lambda i, j: (0, 0)),
            wspec(0), wspec(1), wspec(2), wspec(3),
        ],
        out_specs=[ospec, ospec, ospec],
        out_shape=[oshape, oshape, oshape],
        scratch_shapes=[pltpu.VMEM((tm, d), BF16)],
        compiler_params=_params(vmem, ("parallel", "arbitrary")),
        name="even_in_proj",
    )(x, g.reshape(1, d), w, w, w, w)


def _conv_kernel(main_ref, halo_ref, w_ref, b_ref, lg_ref, lb_ref, o_ref, win_ref, sh_ref, c_ref, *,
                 tiles_per_seq):
    tc, ch = main_ref.shape
    first = (pl.program_id(0) % tiles_per_seq) == 0
    halo = halo_ref[...].astype(F32)
    win_ref[0:CONV_HALO, :] = jnp.where(first, 0.0, halo)
    win_ref[CONV_HALO:, :] = main_ref[...].astype(F32)

    base = CONV_HALO - (CONV_WIDTH - 1)
    sh_rows = sh_ref.shape[1]
    rb = 64
    for cb in range(ch // V7X_LANES):
        cs = slice(cb * V7X_LANES, (cb + 1) * V7X_LANES)
        for s in range(1, V7X_SUBLANES):
            sh_ref[s - 1] = win_ref[s:s + sh_rows, cs]

        def rows(r, carry):
            r0 = pl.multiple_of(r * rb, rb)
            acc = jnp.zeros((rb, V7X_LANES), F32) + b_ref[:, cs]
            for k in range(CONV_WIDTH):
                s = (base + k) % V7X_SUBLANES
                rs = pl.ds(r0 + (base + k - s), rb)
                tap = win_ref[rs, cs] if s == 0 else sh_ref[s - 1, rs, :]
                acc = acc + tap * w_ref[k:k + 1, cs]
            c_ref[pl.ds(r0, rb), cs] = acc
            return carry

        lax.fori_loop(0, tc // rb, rows, 0)

    nr = 32

    def norm(r, carry):
        rs = pl.ds(pl.multiple_of(r * nr, nr), nr)
        y = _layer_norm_rows(c_ref[rs, :], lg_ref[...], lb_ref[...])
        o_ref[rs, :] = _silu(y).astype(o_ref.dtype)
        return carry

    lax.fori_loop(0, tc // nr, norm, 0)


def _conv_ln_silu(glu, seq, w, b, lg, lb, tc=512):
    m, ch = glu.shape
    hb = tc // CONV_HALO
    row = lambda v: v.reshape(1, ch)
    vmem = 2 * tc * ch * 2 * 2 + (tc + CONV_HALO) * ch * 4 + tc * ch * 4 + (4 << 20)
    return pl.pallas_call(
        functools.partial(_conv_kernel, tiles_per_seq=seq // tc),
        grid=(m // tc,),
        in_specs=[
            pl.BlockSpec((tc, ch), lambda i: (i, 0)),
            pl.BlockSpec((CONV_HALO, ch), lambda i: (jnp.maximum(i * hb - 1, 0), 0)),
            pl.BlockSpec((CONV_WIDTH, ch), lambda i: (0, 0)),
            pl.BlockSpec((1, ch), lambda i: (0, 0)),
            pl.BlockSpec((1, ch), lambda i: (0, 0)),
            pl.BlockSpec((1, ch), lambda i: (0, 0)),
        ],
        out_specs=pl.BlockSpec((tc, ch), lambda i: (i, 0)),
        out_shape=jax.ShapeDtypeStruct((m, ch), BF16),
        scratch_shapes=[
            pltpu.VMEM((tc + CONV_HALO, ch), F32),
            pltpu.VMEM((V7X_SUBLANES - 1, tc + CONV_HALO - V7X_SUBLANES, V7X_LANES), F32),
            pltpu.VMEM((tc, ch), F32),
        ],
        compiler_params=_params(vmem, ("parallel",)),
        name="conv_ln_swish",
    )(glu, glu, w, row(b), row(lg), row(lb))


def _gating_kernel(u_ref, v_ref, lg_ref, lb_ref, ws_ref, bs_ref, o_ref, vn_ref):
    tg, ch = u_ref.shape
    gd = ch // GMLP_GROUPS
    nr = 32

    def norm(r, carry):
        rs = pl.ds(pl.multiple_of(r * nr, nr), nr)
        vn_ref[rs, :] = _layer_norm_rows(v_ref[rs, :].astype(F32), lg_ref[...], lb_ref[...]).astype(vn_ref.dtype)
        return carry

    lax.fori_loop(0, tg // nr, norm, 0)

    ii = lax.broadcasted_iota(jnp.int32, (GMLP_CHUNK, GMLP_CHUNK), 0) // CHUNK
    jj = lax.broadcasted_iota(jnp.int32, (GMLP_CHUNK, GMLP_CHUNK), 1) // CHUNK
    keep = jj <= ii
    for g in range(GMLP_GROUPS):
        wg = jnp.where(keep, ws_ref[g], 0.0).astype(BF16)
        bcol = bs_ref[:, g:g + 1]
        cs = slice(g * gd, (g + 1) * gd)
        for c in range(tg // GMLP_CHUNK):
            rs = slice(c * GMLP_CHUNK, (c + 1) * GMLP_CHUNK)
            s = _dot(wg, vn_ref[rs, cs]) + bcol
            o_ref[rs, cs] = (u_ref[rs, cs].astype(F32) * s).astype(o_ref.dtype)


def _gating(gu, gv, lg, lb, w_s, b_s, tg=512):
    m, ch = gu.shape
    row = lambda v: v.reshape(1, ch)
    vmem = 3 * 2 * tg * ch * 2 + tg * ch * 2 + (6 << 20)
    blk = pl.BlockSpec((tg, ch), lambda i: (i, 0))
    return pl.pallas_call(
        _gating_kernel,
        grid=(m // tg,),
        in_specs=[
            blk, blk,
            pl.BlockSpec((1, ch), lambda i: (0, 0)),
            pl.BlockSpec((1, ch), lambda i: (0, 0)),
            pl.BlockSpec(w_s.shape, lambda i: (0, 0, 0)),
            pl.BlockSpec((GMLP_CHUNK, GMLP_GROUPS), lambda i: (0, 0)),
        ],
        out_specs=blk,
        out_shape=jax.ShapeDtypeStruct((m, ch), BF16),
        scratch_shapes=[pltpu.VMEM((tg, ch), BF16)],
        compiler_params=_params(vmem, ("parallel",)),
        name="spatial_gating",
    )(gu, gv, row(lg), row(lb), w_s, b_s.T)


def _proj_res_kernel(*refs, n_parts):
    a_refs = refs[:n_parts]
    w_refs = refs[n_parts:2 * n_parts]
    res_ref, o_ref = refs[2 * n_parts], refs[2 * n_parts + 1]
    acc = res_ref[...]
    for a_ref, w_ref in zip(a_refs, w_refs):
        acc = acc + _dot(a_ref[...], w_ref[...])
    o_ref[...] = acc


def _proj_res(parts, w, res, tn=512):
    m, n = res.shape
    kp = parts[0].shape[1]
    n_parts = len(parts)
    tm = ROW_TILE
    vmem = n_parts * 2 * (tm * kp * 2 + kp * tn * 2) + 4 * tm * tn * 4 + 2 * tm * tn * 4 + (4 << 20)
    a_specs = [pl.BlockSpec((tm, kp), lambda i, j: (i, 0)) for _ in parts]
    w_specs = [pl.BlockSpec((kp, tn), lambda i, j, p=p: (p, j)) for p in range(n_parts)]
    return pl.pallas_call(
        functools.partial(_proj_res_kernel, n_parts=n_parts),
        grid=(m // tm, n // tn),
        in_specs=a_specs + w_specs + [pl.BlockSpec((tm, tn), lambda i, j: (i, j))],
        out_specs=pl.BlockSpec((tm, tn), lambda i, j: (i, j)),
        out_shape=jax.ShapeDtypeStruct((m, n), F32),
        compiler_params=_params(vmem, ("parallel", "arbitrary")),
        name="proj_residual",
    )(*parts, *([w] * n_parts), res)


def _ffn_kernel(x_ref, g_ref, wg_ref, wu_ref, wd_ref, fg_ref, o_ref, h_ref, *, final_norm, row_chunk):
    f = pl.program_id(1)

    @pl.when(f == 0)
    def _():
        _norm_rows_into(x_ref, g_ref, h_ref)
        o_ref[...] = x_ref[...]

    for r in range(x_ref.shape[0] // row_chunk):
        rs = slice(r * row_chunk, (r + 1) * row_chunk)
        h = h_ref[rs, :]
        gate = _dot(h, wg_ref[...])
        up = _dot(h, wu_ref[...])
        act = (_silu(gate) * up).astype(BF16)
        o_ref[rs, :] += _dot(act, wd_ref[...])

    if final_norm:
        @pl.when(f == pl.num_programs(1) - 1)
        def _():
            nr = 128

            def body(r, carry):
                rs = pl.ds(pl.multiple_of(r * nr, nr), nr)
                o_ref[rs, :] = _rms_rows(o_ref[rs, :], fg_ref[...])
                return carry

            lax.fori_loop(0, o_ref.shape[0] // nr, body, 0)


def _ffn(x, g, w_gate_up, w_down, final_g=None, tf=512, row_chunk=256):
    m, d = x.shape
    hidden = w_down.shape[0]
    nf = hidden // tf
    tm = ROW_TILE
    final_norm = final_g is not None
    fg = (final_g if final_norm else g).reshape(1, d)
    vmem = (tm * d * 4 + tm * d * 2 + 2 * tm * d * 4 + 3 * 2 * d * tf * 2
            + 2 * row_chunk * tf * 4 + row_chunk * d * 4 + (6 << 20))
    return pl.pallas_call(
        functools.partial(_ffn_kernel, final_norm=final_norm, row_chunk=row_chunk),
        grid=(m // tm, nf),
        in_specs=[
            pl.BlockSpec((tm, d), lambda i, f: (i, 0), pipeline_mode=pl.Buffered(1)),
            pl.BlockSpec((1, d), lambda i, f: (0, 0)),
            pl.BlockSpec((d, tf), lambda i, f: (0, f)),
            pl.BlockSpec((d, tf), lambda i, f: (0, nf + f)),
            pl.BlockSpec((tf, d), lambda i, f: (f, 0)),
            pl.BlockSpec((1, d), lambda i, f: (0, 0)),
        ],
        out_specs=pl.BlockSpec((tm, d), lambda i, f: (i, 0)),
        out_shape=jax.ShapeDtypeStruct((m, d), F32),
        scratch_shapes=[pltpu.VMEM((tm, d), BF16)],
        compiler_params=_params(vmem, ("parallel", "arbitrary")),
        name="swiglu_ffn",
    )(x, g.reshape(1, d), w_gate_up, w_gate_up, w_down, fg)


def _diff_attn_kernel(q_ref, k_ref, v_ref, lq1_ref, lk1_ref, lq2_ref, lk2_ref, sg_ref, o_ref,
                      acc_ref, m_ref, l_ref, s_ref, *, lambda_init, tk):
    tq, hd2 = q_ref.shape
    hd = hd2 // 2
    qi = pl.program_id(2)
    q = q_ref[...]

    m_ref[...] = jnp.full(m_ref.shape, -jnp.inf, F32)
    l_ref[...] = jnp.zeros(l_ref.shape, F32)
    acc_ref[...] = jnp.zeros(acc_ref.shape, F32)

    def key_rows(kb):
        return pl.ds(pl.multiple_of(kb * tk, tk), tk)

    def scores(kb):
        kblk = k_ref[key_rows(kb), :]
        return [lax.dot_general(q[:, c * hd:(c + 1) * hd], kblk[:, c * hd:(c + 1) * hd],
                                (((1,), (1,)), ((), ())), preferred_element_type=F32) for c in range(2)]

    def consume(kb, masked):
        vblk = v_ref[key_rows(kb), :]
        if masked:
            qc_id = lax.broadcasted_iota(jnp.int32, (tq, tk), 0) // CHUNK
            kc_id = lax.broadcasted_iota(jnp.int32, (tq, tk), 1) // CHUNK
            keep = kc_id <= qc_id
        for c in range(2):
            s = s_ref[c]
            if masked:
                s = jnp.where(keep, s, -jnp.inf)
            m_prev = m_ref[c]
            m_new = jnp.maximum(m_prev, jnp.max(s, axis=-1, keepdims=True))
            alpha = jnp.exp2(m_prev - m_new)
            p = jnp.exp2(s - jnp.tile(m_new, (1, tk // V7X_LANES)))
            l_ref[c] = alpha * l_ref[c] + jnp.sum(p, axis=-1, keepdims=True)
            acc_ref[c] = jnp.tile(alpha, (1, hd2 // V7X_LANES)) * acc_ref[c] + _dot(p.astype(BF16), vblk)
            m_ref[c] = m_new

    def put_scores(s):
        s_ref[0] = s[0]
        s_ref[1] = s[1]

    put_scores(scores(0))

    def full_block(kb, carry):
        s_next = scores(kb + 1)
        consume(kb, False)
        put_scores(s_next)
        return carry

    lax.fori_loop(0, qi, full_block, 0)
    consume(qi, True)

    lam = (jnp.exp(jnp.sum(lq1_ref[...] * lk1_ref[...], axis=-1, keepdims=True))
           - jnp.exp(jnp.sum(lq2_ref[...] * lk2_ref[...], axis=-1, keepdims=True)) + lambda_init)
    reps = (1, hd2 // V7X_LANES)
    o = acc_ref[0] / jnp.tile(l_ref[0], reps) - lam * (acc_ref[1] / jnp.tile(l_ref[1], reps))
    o = _rms_rows(o, sg_ref[...]) * (1.0 - lambda_init)
    o_ref[...] = o.astype(o_ref.dtype)


def _diff_attn(qkv, batch, seq, lq1, lk1, lq2, lk2, subln_g, lambda_init, tq=512):
    m = qkv.shape[0]
    hw = qkv.shape[1] // (3 * DIFF_HEADS)
    nq = seq // tq
    row = lambda v: v.reshape(1, -1)
    vmem = 2 * tq * hw * 2 + 2 * 2 * seq * hw * 2 + 2 * tq * hw * 2 + 2 * tq * hw * 4 + 8 * tq * tq * 4 + (4 << 20)
    lspec = pl.BlockSpec((1, hw // 2), lambda b, h, i: (0, 0))
    return pl.pallas_call(
        functools.partial(_diff_attn_kernel, lambda_init=lambda_init, tk=tq),
        grid=(batch, DIFF_HEADS, nq),
        in_specs=[
            pl.BlockSpec((tq, hw), lambda b, h, i: (b * nq + i, h)),
            pl.BlockSpec((seq, hw), lambda b, h, i: (b, DIFF_HEADS + h)),
            pl.BlockSpec((seq, hw), lambda b, h, i: (b, 2 * DIFF_HEADS + h)),
            lspec, lspec, lspec, lspec,
            pl.BlockSpec((1, hw), lambda b, h, i: (0, 0)),
        ],
        out_specs=pl.BlockSpec((tq, hw), lambda b, h, i: (b * nq + i, h)),
        out_shape=jax.ShapeDtypeStruct((m, DIFF_HEADS * hw), BF16),
        scratch_shapes=[
            pltpu.VMEM((2, tq, hw), F32),
            pltpu.VMEM((2, tq, V7X_LANES), F32),
            pltpu.VMEM((2, tq, V7X_LANES), F32),
            pltpu.VMEM((2, tq, tq), F32),
        ],
        compiler_params=_params(vmem, ("parallel", "parallel", "arbitrary")),
        name="diff_attention",
    )(qkv, qkv, qkv, row(lq1), row(lk1), row(lq2), row(lk2), row(subln_g))


def kernel(x, w_in_even, conv_w, conv_b, conv_ln_g, conv_ln_b, gmlp_ln_g, gmlp_ln_b, gmlp_w_s, gmlp_b_s,
           w_out_even, w_qkv_odd, w_o_odd, lambda_q1, lambda_k1, lambda_q2, lambda_k2, subln_g, mix_norm_g,
           ffn_norm_g, w_gate_up, w_down, final_norm_g):
    batch, seq, d = x.shape
    depth = mix_norm_g.shape[0]
    xf = x.reshape(batch * seq, d)
    head_dim = d // (2 * DIFF_HEADS)
    for layer in range(depth):
        i = layer // 2
        if layer % 2 == 0:
            glu, gu, gv = _in_proj(xf, mix_norm_g[layer], w_in_even[i].astype(BF16))
            a = _conv_ln_silu(glu, seq, conv_w[i], conv_b[i], conv_ln_g[i], conv_ln_b[i])
            b = _gating(gu, gv, gmlp_ln_g[i], gmlp_ln_b[i], gmlp_w_s[i], gmlp_b_s[i])
            xf = _proj_res([a, b], w_out_even[i].astype(BF16), xf)
        else:
            lambda_init = 0.8 - 0.6 * math.exp(-0.3 * layer)
            colscale = jnp.concatenate([
                jnp.full((1, d), head_dim ** -0.5 * LOG2E, F32), jnp.ones((1, 2 * d), F32)], axis=1)
            qkv = _norm_matmul(xf, mix_norm_g[layer], w_qkv_odd[i].astype(BF16), colscale)
            o = _diff_attn(qkv, batch, seq, lambda_q1[i], lambda_k1[i], lambda_q2[i], lambda_k2[i],
                           subln_g[i], lambda_init)
            xf = _proj_res([o], w_o_odd[i].astype(BF16), xf)
        xf = _ffn(xf, ffn_norm_g[layer], w_gate_up[layer].astype(BF16), w_down[layer].astype(BF16),
                  final_g=final_norm_g if layer == depth - 1 else None)
    return xf.reshape(batch, seq, d)
```

```python
import functools
import math

import jax
import jax.numpy as jnp
from jax import lax
from jax.experimental import pallas as pl
from jax.experimental.pallas import tpu as pltpu

F32 = jnp.float32
BF16 = jnp.bfloat16

EPS = 1e-6
CHUNK = 64
CONV_WIDTH = 31
GMLP_GROUPS = 8
GMLP_CHUNK = 128
DIFF_HEADS = 8
LOG2E = 1.4426950408889634

V7X_LANES = 128
V7X_SUBLANES = 8
V7X_VMEM_LIMIT_BYTES = 56 * 1024 * 1024

ROW_TILE = 1024
ROW_CHUNK = 256
NORM_ROWS = 128
CONV_HALO = 32


def _dot(a, b):
    return jnp.dot(a, b, preferred_element_type=F32)


def _rms_rows(x, g):
    ms = jnp.mean(x * x, axis=-1, keepdims=True)
    return x * lax.rsqrt(ms + EPS) * g


def _layer_norm_rows(x, g, b):
    mu = jnp.mean(x, axis=-1, keepdims=True)
    d = x - mu
    var = jnp.mean(d * d, axis=-1, keepdims=True)
    return d * lax.rsqrt(var + EPS) * g + b


def _silu(x):
    return x * jax.nn.sigmoid(x)


def _gelu_exact(x):
    return 0.5 * x * (1.0 + lax.erf(x * (1.0 / math.sqrt(2.0))))


def _norm_rows_into(x_ref, g_ref, h_ref):
    def body(r, carry):
        rs = pl.ds(pl.multiple_of(r * NORM_ROWS, NORM_ROWS), NORM_ROWS)
        h_ref[rs, :] = _rms_rows(x_ref[rs, :], g_ref[...]).astype(h_ref.dtype)
        return carry

    lax.fori_loop(0, x_ref.shape[0] // NORM_ROWS, body, 0)


def _params(vmem_bytes, semantics):
    return pltpu.CompilerParams(
        dimension_semantics=semantics,
        vmem_limit_bytes=min(int(vmem_bytes), V7X_VMEM_LIMIT_BYTES),
    )


def _scaled_matmul_kernel(h_ref, w_ref, cs_ref, o_ref):
    for r in range(h_ref.shape[0] // ROW_CHUNK):
        rs = slice(r * ROW_CHUNK, (r + 1) * ROW_CHUNK)
        o_ref[rs, :] = (_dot(h_ref[rs, :], w_ref[...]) * cs_ref[...]).astype(o_ref.dtype)


def _scaled_matmul(h, w, colscale, tn=2048):
    m, d = h.shape
    n = w.shape[1]
    tm = ROW_TILE
    vmem = 2 * tm * d * 2 + 2 * d * tn * 2 + 2 * tm * tn * 2 + 2 * ROW_CHUNK * tn * 4 + (4 << 20)
    return pl.pallas_call(
        _scaled_matmul_kernel,
        grid=(m // tm, n // tn),
        in_specs=[
            pl.BlockSpec((tm, d), lambda i, j: (i, 0)),
            pl.BlockSpec((d, tn), lambda i, j: (0, j)),
            pl.BlockSpec((1, tn), lambda i, j: (0, j)),
        ],
        out_specs=pl.BlockSpec((tm, tn), lambda i, j: (i, j)),
        out_shape=jax.ShapeDtypeStruct((m, n), BF16),
        compiler_params=_params(vmem, ("parallel", "arbitrary")),
        name="qkv_proj",
    )(h, w, colscale)


def _in_proj_kernel(*refs, normalize):
    if normalize:
        x_ref, g_ref, wa_ref, wg_ref, wu_ref, wv_ref, glu_ref, gu_ref, gv_ref, h_ref = refs

        @pl.when(pl.program_id(1) == 0)
        def _():
            _norm_rows_into(x_ref, g_ref, h_ref)
    else:
        h_ref, wa_ref, wg_ref, wu_ref, wv_ref, glu_ref, gu_ref, gv_ref = refs

    h = h_ref[...]
    glu_ref[...] = (_dot(h, wa_ref[...]) * jax.nn.sigmoid(_dot(h, wg_ref[...]))).astype(glu_ref.dtype)
    gu_ref[...] = _gelu_exact(_dot(h, wu_ref[...])).astype(gu_ref.dtype)
    gv_ref[...] = _gelu_exact(_dot(h, wv_ref[...])).astype(gv_ref.dtype)


def _in_proj(x, g, w, tn=512):
    m, d = x.shape
    sec = w.shape[1] // 4
    nb = sec // tn
    tm = ROW_TILE
    normalize = g is not None
    vmem = (2 * tm * d * x.dtype.itemsize + tm * d * 2 + 4 * 2 * d * tn * 2 + 3 * 2 * tm * tn * 2
            + 4 * tm * tn * 4 + (4 << 20))
    wspec = lambda s: pl.BlockSpec((d, tn), lambda i, j: (0, s * nb + j))
    ospec = pl.BlockSpec((tm, tn), lambda i, j: (i, j))
    oshape = jax.ShapeDtypeStruct((m, sec), BF16)
    x_spec = pl.BlockSpec((tm, d), lambda i, j: (i, 0))
    if normalize:
        in_specs = [x_spec, pl.BlockSpec((1, d), lambda i, j: (0, 0))]
        operands = (x, g.reshape(1, d))
        scratch = [pltpu.VMEM((tm, d), BF16)]
    else:
        in_specs, operands, scratch = [x_spec], (x,), []
    return pl.pallas_call(
        functools.partial(_in_proj_kernel, normalize=normalize),
        grid=(m // tm, nb),
        in_specs=in_specs + [wspec(0), wspec(1), wspec(2), wspec(3)],
        out_specs=[ospec, ospec, ospec],
        out_shape=[oshape, oshape, oshape],
        scratch_shapes=scratch,
        compiler_params=_params(vmem, ("parallel", "arbitrary")),
        name="even_in_proj",
    )(*operands, w, w, w, w)


def _conv_kernel(main_ref, halo_ref, w_ref, b_ref, lg_ref, lb_ref, o_ref, win_ref, sh_ref, c_ref, *,
                 tiles_per_seq):
    tc, ch = main_ref.shape
    first = (pl.program_id(0) % tiles_per_seq) == 0
    halo = halo_ref[...].astype(F32)
    win_ref[0:CONV_HALO, :] = jnp.where(first, 0.0, halo)
    win_ref[CONV_HALO:, :] = main_ref[...].astype(F32)

    base = CONV_HALO - (CONV_WIDTH - 1)
    sh_rows = sh_ref.shape[1]
    rb = 128
    for cb in range(ch // V7X_LANES):
        cs = slice(cb * V7X_LANES, (cb + 1) * V7X_LANES)
        for s in range(1, V7X_SUBLANES):
            sh_ref[s - 1] = win_ref[s:s + sh_rows, cs]

        def rows(r, carry):
            r0 = pl.multiple_of(r * rb, rb)
            acc = jnp.zeros((rb, V7X_LANES), F32) + b_ref[:, cs]
            for k in range(CONV_WIDTH):
                s = (base + k) % V7X_SUBLANES
                rs = pl.ds(r0 + (base + k - s), rb)
                tap = win_ref[rs, cs] if s == 0 else sh_ref[s - 1, rs, :]
                acc = acc + tap * w_ref[k:k + 1, cs]
            c_ref[pl.ds(r0, rb), cs] = acc
            return carry

        lax.fori_loop(0, tc // rb, rows, 0)

    nr = NORM_ROWS

    def norm(r, carry):
        rs = pl.ds(pl.multiple_of(r * nr, nr), nr)
        y = _layer_norm_rows(c_ref[rs, :], lg_ref[...], lb_ref[...])
        o_ref[rs, :] = _silu(y).astype(o_ref.dtype)
        return carry

    lax.fori_loop(0, tc // nr, norm, 0)


def _conv_ln_silu(glu, seq, w, b, lg, lb, tc=512):
    m, ch = glu.shape
    hb = tc // CONV_HALO
    row = lambda v: v.reshape(1, ch)
    vmem = 2 * tc * ch * 2 * 2 + (tc + CONV_HALO) * ch * 4 + tc * ch * 4 + (4 << 20)
    return pl.pallas_call(
        functools.partial(_conv_kernel, tiles_per_seq=seq // tc),
        grid=(m // tc,),
        in_specs=[
            pl.BlockSpec((tc, ch), lambda i: (i, 0)),
            pl.BlockSpec((CONV_HALO, ch), lambda i: (jnp.maximum(i * hb - 1, 0), 0)),
            pl.BlockSpec((CONV_WIDTH, ch), lambda i: (0, 0)),
            pl.BlockSpec((1, ch), lambda i: (0, 0)),
            pl.BlockSpec((1, ch), lambda i: (0, 0)),
            pl.BlockSpec((1, ch), lambda i: (0, 0)),
        ],
        out_specs=pl.BlockSpec((tc, ch), lambda i: (i, 0)),
        out_shape=jax.ShapeDtypeStruct((m, ch), BF16),
        scratch_shapes=[
            pltpu.VMEM((tc + CONV_HALO, ch), F32),
            pltpu.VMEM((V7X_SUBLANES - 1, tc + CONV_HALO - V7X_SUBLANES, V7X_LANES), F32),
            pltpu.VMEM((tc, ch), F32),
        ],
        compiler_params=_params(vmem, ("parallel",)),
        name="conv_ln_swish",
    )(glu, glu, w, row(b), row(lg), row(lb))


def _gating_kernel(u_ref, v_ref, lg_ref, lb_ref, ws_ref, bs_ref, o_ref, vn_ref):
    tg, ch = u_ref.shape
    gd = ch // GMLP_GROUPS
    nr = NORM_ROWS

    def norm(r, carry):
        rs = pl.ds(pl.multiple_of(r * nr, nr), nr)
        vn_ref[rs, :] = _layer_norm_rows(v_ref[rs, :].astype(F32), lg_ref[...], lb_ref[...]).astype(vn_ref.dtype)
        return carry

    lax.fori_loop(0, tg // nr, norm, 0)

    ii = lax.broadcasted_iota(jnp.int32, (GMLP_CHUNK, GMLP_CHUNK), 0) // CHUNK
    jj = lax.broadcasted_iota(jnp.int32, (GMLP_CHUNK, GMLP_CHUNK), 1) // CHUNK
    keep = jj <= ii
    for g in range(GMLP_GROUPS):
        wg = jnp.where(keep, ws_ref[g], 0.0).astype(BF16)
        bcol = bs_ref[:, g:g + 1]
        cs = slice(g * gd, (g + 1) * gd)
        for c in range(tg // GMLP_CHUNK):
            rs = slice(c * GMLP_CHUNK, (c + 1) * GMLP_CHUNK)
            s = _dot(wg, vn_ref[rs, cs]) + bcol
            o_ref[rs, cs] = (u_ref[rs, cs].astype(F32) * s).astype(o_ref.dtype)


def _gating(gu, gv, lg, lb, w_s, b_s, tg=512):
    m, ch = gu.shape
    row = lambda v: v.reshape(1, ch)
    vmem = 3 * 2 * tg * ch * 2 + tg * ch * 2 + (6 << 20)
    blk = pl.BlockSpec((tg, ch), lambda i: (i, 0))
    return pl.pallas_call(
        _gating_kernel,
        grid=(m // tg,),
        in_specs=[
            blk, blk,
            pl.BlockSpec((1, ch), lambda i: (0, 0)),
            pl.BlockSpec((1, ch), lambda i: (0, 0)),
            pl.BlockSpec(w_s.shape, lambda i: (0, 0, 0)),
            pl.BlockSpec((GMLP_CHUNK, GMLP_GROUPS), lambda i: (0, 0)),
        ],
        out_specs=blk,
        out_shape=jax.ShapeDtypeStruct((m, ch), BF16),
        scratch_shapes=[pltpu.VMEM((tg, ch), BF16)],
        compiler_params=_params(vmem, ("parallel",)),
        name="spatial_gating",
    )(gu, gv, row(lg), row(lb), w_s, b_s.T)


def _proj_norm_kernel(*refs, n_parts, last):
    a_refs = refs[:n_parts]
    w_ref, res_ref, g_ref = refs[n_parts:n_parts + 3]
    out_refs = refs[n_parts + 3:]
    kp = a_refs[0].shape[1]
    acc = res_ref[...]
    for p, a_ref in enumerate(a_refs):
        acc = acc + _dot(a_ref[...], w_ref[p * kp:(p + 1) * kp, :])
    normed = _rms_rows(acc, g_ref[...])
    if last:
        out_refs[0][...] = normed
    else:
        out_refs[0][...] = acc
        out_refs[1][...] = normed.astype(out_refs[1].dtype)


def _proj_norm(parts, w, res, g, tm, last=False):
    m, n = res.shape
    kp = parts[0].shape[1]
    n_parts = len(parts)
    vmem = (w.size * 2 + n_parts * 2 * tm * kp * 2 + 2 * tm * n * 4 + 2 * tm * n * 4 + 2 * tm * n * 2
            + 3 * tm * n * 4 + (4 << 20))
    row_blk = lambda width: pl.BlockSpec((tm, width), lambda i: (i, 0))
    if last:
        out_specs, out_shape = [row_blk(n)], [jax.ShapeDtypeStruct((m, n), F32)]
    else:
        out_specs = [row_blk(n), row_blk(n)]
        out_shape = [jax.ShapeDtypeStruct((m, n), F32), jax.ShapeDtypeStruct((m, n), BF16)]
    return pl.pallas_call(
        functools.partial(_proj_norm_kernel, n_parts=n_parts, last=last),
        grid=(m // tm,),
        in_specs=[row_blk(kp) for _ in parts] + [
            pl.BlockSpec(w.shape, lambda i: (0, 0), pipeline_mode=pl.Buffered(1)),
            row_blk(n),
            pl.BlockSpec((1, n), lambda i: (0, 0)),
        ],
        out_specs=out_specs,
        out_shape=out_shape,
        compiler_params=_params(vmem, ("parallel",)),
        name="proj_residual_norm",
    )(*parts, w, res, g.reshape(1, n))


def _gate_up_kernel(h_ref, w_ref, o_ref):
    tf = o_ref.shape[1]
    for r in range(h_ref.shape[0] // ROW_CHUNK):
        rs = slice(r * ROW_CHUNK, (r + 1) * ROW_CHUNK)
        gu = _dot(h_ref[rs, :], w_ref[...])
        o_ref[rs, :] = (_silu(gu[:, :tf]) * gu[:, tf:]).astype(o_ref.dtype)


def _gate_up(h, w_chunked, tf):
    m, d = h.shape
    hidden = w_chunked.shape[1] // 2
    tm = ROW_TILE
    vmem = 2 * tm * d * 2 + 2 * d * 2 * tf * 2 + 2 * tm * tf * 2 + 3 * ROW_CHUNK * 2 * tf * 4 + (4 << 20)
    return pl.pallas_call(
        _gate_up_kernel,
        grid=(m // tm, hidden // tf),
        in_specs=[
            pl.BlockSpec((tm, d), lambda i, f: (i, 0)),
            pl.BlockSpec((d, 2 * tf), lambda i, f: (0, f)),
        ],
        out_specs=pl.BlockSpec((tm, tf), lambda i, f: (i, f)),
        out_shape=jax.ShapeDtypeStruct((m, hidden), BF16),
        compiler_params=_params(vmem, ("parallel", "arbitrary")),
        name="swiglu_gate_up",
    )(h, w_chunked)


def _chunk_gate_up(w_gate_up, tf):
    d, two_hidden = w_gate_up.shape
    nf = two_hidden // (2 * tf)
    w = w_gate_up.reshape(d, 2, nf, tf).transpose(0, 2, 1, 3)
    return w.reshape(d, two_hidden).astype(BF16)


def _diff_attn_kernel(q_ref, k_ref, v_ref, lq1_ref, lk1_ref, lq2_ref, lk2_ref, sg_ref, o_ref,
                      acc_ref, m_ref, l_ref, s_ref, *, lambda_init, tk):
    tq, hd2 = q_ref.shape
    hd = hd2 // 2
    qi = pl.program_id(2)
    q = q_ref[...]

    m_ref[...] = jnp.full(m_ref.shape, -jnp.inf, F32)
    l_ref[...] = jnp.zeros(l_ref.shape, F32)
    acc_ref[...] = jnp.zeros(acc_ref.shape, F32)

    def key_rows(kb):
        return pl.ds(pl.multiple_of(kb * tk, tk), tk)

    def scores(kb):
        kblk = k_ref[key_rows(kb), :]
        return [lax.dot_general(q[:, c * hd:(c + 1) * hd], kblk[:, c * hd:(c + 1) * hd],
                                (((1,), (1,)), ((), ())), preferred_element_type=F32) for c in range(2)]

    def consume(kb, masked):
        vblk = v_ref[key_rows(kb), :]
        if masked:
            qc_id = lax.broadcasted_iota(jnp.int32, (tq, tk), 0) // CHUNK
            kc_id = lax.broadcasted_iota(jnp.int32, (tq, tk), 1) // CHUNK
            keep = kc_id <= qc_id
        for c in range(2):
            s = s_ref[c]
            if masked:
                s = jnp.where(keep, s, -jnp.inf)
            m_prev = m_ref[c]
            m_new = jnp.maximum(m_prev, jnp.max(s, axis=-1, keepdims=True))
            alpha = jnp.exp2(m_prev - m_new)
            p = jnp.exp2(s - jnp.tile(m_new, (1, tk // V7X_LANES)))
            l_ref[c] = alpha * l_ref[c] + jnp.sum(p, axis=-1, keepdims=True)
            acc_ref[c] = jnp.tile(alpha, (1, hd2 // V7X_LANES)) * acc_ref[c] + _dot(p.astype(BF16), vblk)
            m_ref[c] = m_new

    def put_scores(s):
        s_ref[0] = s[0]
        s_ref[1] = s[1]

    put_scores(scores(0))

    def full_block(kb, carry):
        s_next = scores(kb + 1)
        consume(kb, False)
        put_scores(s_next)
        return carry

    lax.fori_loop(0, qi, full_block, 0)
    consume(qi, True)

    lam = (jnp.exp(jnp.sum(lq1_ref[...] * lk1_ref[...], axis=-1, keepdims=True))
           - jnp.exp(jnp.sum(lq2_ref[...] * lk2_ref[...], axis=-1, keepdims=True)) + lambda_init)
    reps = (1, hd2 // V7X_LANES)
    o = acc_ref[0] / jnp.tile(l_ref[0], reps) - lam * (acc_ref[1] / jnp.tile(l_ref[1], reps))
    o = _rms_rows(o, sg_ref[...]) * (1.0 - lambda_init)
    o_ref[...] = o.astype(o_ref.dtype)


def _diff_attn(qkv, batch, seq, lq1, lk1, lq2, lk2, subln_g, lambda_init, tq=512):
    m = qkv.shape[0]
    hw = qkv.shape[1] // (3 * DIFF_HEADS)
    nq = seq // tq
    row = lambda v: v.reshape(1, -1)
    vmem = 2 * tq * hw * 2 + 2 * 2 * seq * hw * 2 + 2 * tq * hw * 2 + 2 * tq * hw * 4 + 8 * tq * tq * 4 + (4 << 20)
    lspec = pl.BlockSpec((1, hw // 2), lambda b, h, i: (0, 0))
    return pl.pallas_call(
        functools.partial(_diff_attn_kernel, lambda_init=lambda_init, tk=tq),
        grid=(batch, DIFF_HEADS, nq),
        in_specs=[
            pl.BlockSpec((tq, hw), lambda b, h, i: (b * nq + i, h)),
            pl.BlockSpec((seq, hw), lambda b, h, i: (b, DIFF_HEADS + h)),
            pl.BlockSpec((seq, hw), lambda b, h, i: (b, 2 * DIFF_HEADS + h)),
            lspec, lspec, lspec, lspec,
            pl.BlockSpec((1, hw), lambda b, h, i: (0, 0)),
        ],
        out_specs=pl.BlockSpec((tq, hw), lambda b, h, i: (b * nq + i, h)),
        out_shape=jax.ShapeDtypeStruct((m, DIFF_HEADS * hw), BF16),
        scratch_shapes=[
            pltpu.VMEM((2, tq, hw), F32),
            pltpu.VMEM((2, tq, V7X_LANES), F32),
            pltpu.VMEM((2, tq, V7X_LANES), F32),
            pltpu.VMEM((2, tq, tq), F32),
        ],
        compiler_params=_params(vmem, ("parallel", "parallel", "arbitrary")),
        name="diff_attention",
    )(qkv, qkv, qkv, row(lq1), row(lk1), row(lq2), row(lk2), row(subln_g))


def kernel(x, w_in_even, conv_w, conv_b, conv_ln_g, conv_ln_b, gmlp_ln_g, gmlp_ln_b, gmlp_w_s, gmlp_b_s,
           w_out_even, w_qkv_odd, w_o_odd, lambda_q1, lambda_k1, lambda_q2, lambda_k2, subln_g, mix_norm_g,
           ffn_norm_g, w_gate_up, w_down, final_norm_g):
    batch, seq, d = x.shape
    depth = mix_norm_g.shape[0]
    hidden = w_down.shape[1]
    head_dim = d // (2 * DIFF_HEADS)
    ffn_chunk = hidden // 4 if (hidden // 4) % V7X_LANES == 0 else V7X_LANES
    xf = x.reshape(batch * seq, d)
    h = None
    for layer in range(depth):
        i = layer // 2
        if layer % 2 == 0:
            src, gain = (xf, mix_norm_g[layer]) if h is None else (h, None)
            glu, gu, gv = _in_proj(src, gain, w_in_even[i].astype(BF16))
            a = _conv_ln_silu(glu, seq, conv_w[i], conv_b[i], conv_ln_g[i], conv_ln_b[i])
            b = _gating(gu, gv, gmlp_ln_g[i], gmlp_ln_b[i], gmlp_w_s[i], gmlp_b_s[i])
            xf, h = _proj_norm([a, b], w_out_even[i].astype(BF16), xf, ffn_norm_g[layer], tm=256)
        else:
            lambda_init = 0.8 - 0.6 * math.exp(-0.3 * layer)
            colscale = jnp.concatenate([
                jnp.full((1, d), head_dim ** -0.5 * LOG2E, F32), jnp.ones((1, 2 * d), F32)], axis=1)
            qkv = _scaled_matmul(h, w_qkv_odd[i].astype(BF16), colscale)
            o = _diff_attn(qkv, batch, seq, lambda_q1[i], lambda_k1[i], lambda_q2[i], lambda_k2[i],
                           subln_g[i], lambda_init)
            xf, h = _proj_norm([o], w_o_odd[i].astype(BF16), xf, ffn_norm_g[layer], tm=512)
        act = _gate_up(h, _chunk_gate_up(w_gate_up[layer], ffn_chunk), ffn_chunk)
        if layer == depth - 1:
            (xf,) = _proj_norm([act], w_down[layer].astype(BF16), xf, final_norm_g, tm=256, last=True)
        else:
            xf, h = _proj_norm([act], w_down[layer].astype(BF16), xf, mix_norm_g[layer + 1], tm=256)
    return xf.reshape(batch, seq, d)
```

```python
import functools
import math

import jax
import jax.numpy as jnp
from jax import lax
from jax.experimental import pallas as pl
from jax.experimental.pallas import tpu as pltpu

F32 = jnp.float32
BF16 = jnp.bfloat16

EPS = 1e-6
CHUNK = 64
CONV_WIDTH = 31
GMLP_GROUPS = 8
GMLP_CHUNK = 128
DIFF_HEADS = 8
LOG2E = 1.4426950408889634

V7X_LANES = 128
V7X_SUBLANES = 8
V7X_VMEM_LIMIT_BYTES = 54 * 1024 * 1024

ROW_TILE = 1024
ROW_CHUNK = 256
DOWN_ROWS = 256
OUT_ROWS = 256
NORM_ROWS = 128
CONV_HALO = 32


def _dot(a, b):
    return jnp.dot(a, b, preferred_element_type=F32)


def _rms_rows(x, g):
    ms = jnp.mean(x * x, axis=-1, keepdims=True)
    return x * lax.rsqrt(ms + EPS) * g


def _layer_norm_rows(x, g, b):
    mu = jnp.mean(x, axis=-1, keepdims=True)
    d = x - mu
    var = jnp.mean(d * d, axis=-1, keepdims=True)
    return d * lax.rsqrt(var + EPS) * g + b


def _silu(x):
    return x * jax.nn.sigmoid(x)


def _gelu_exact(x):
    return 0.5 * x * (1.0 + lax.erf(x * (1.0 / math.sqrt(2.0))))


def _norm_rows_into(x_ref, g_ref, h_ref):
    def body(r, carry):
        rs = pl.ds(pl.multiple_of(r * NORM_ROWS, NORM_ROWS), NORM_ROWS)
        h_ref[rs, :] = _rms_rows(x_ref[rs, :], g_ref[...]).astype(h_ref.dtype)
        return carry

    lax.fori_loop(0, x_ref.shape[0] // NORM_ROWS, body, 0)


def _params(vmem_bytes, semantics):
    return pltpu.CompilerParams(
        dimension_semantics=semantics,
        vmem_limit_bytes=min(int(vmem_bytes), V7X_VMEM_LIMIT_BYTES),
    )


BF16_ROW_TILE = 16


def _can_carry(w, steps):
    rows = w.shape[0] * w.shape[1]
    return rows > 0 and rows % steps == 0 and (rows // steps) % BF16_ROW_TILE == 0


def _carrying(body, n_in, n_out, n_casts):
    def wrapped(*refs):
        srcs = refs[n_in:n_in + n_casts]
        dsts = refs[n_in + n_casts + n_out:n_in + 2 * n_casts + n_out]
        for src, dst in zip(srcs, dsts):
            dst[...] = src[...].astype(dst.dtype)
        body(*refs[:n_in], *refs[n_in + n_casts:n_in + n_casts + n_out], *refs[n_in + 2 * n_casts + n_out:])

    return wrapped


def _cast_specs(casts, steps, step_index):
    flat = [w.reshape(-1, w.shape[-1]) for w in casts]
    specs = [pl.BlockSpec((w.shape[0] // steps, w.shape[1]), lambda *ids: (step_index(*ids), 0)) for w in flat]
    shapes = [jax.ShapeDtypeStruct(w.shape, BF16) for w in flat]
    vmem = sum(2 * (w.shape[0] // steps) * w.shape[1] * (4 + 2) for w in flat)
    return flat, specs, specs, shapes, vmem


def _split_outputs(outs, n_out, casts):
    return list(outs[:n_out]), [o.reshape(w.shape) for o, w in zip(outs[n_out:], casts)]


def _scaled_matmul_kernel(h_ref, w_ref, cs_ref, o_ref):
    for r in range(h_ref.shape[0] // ROW_CHUNK):
        rs = slice(r * ROW_CHUNK, (r + 1) * ROW_CHUNK)
        o_ref[rs, :] = (_dot(h_ref[rs, :], w_ref[...]) * cs_ref[...]).astype(o_ref.dtype)


def _scaled_matmul(h, w_stack, idx, colscale, tn=2048):
    m, d = h.shape
    n = w_stack.shape[2]
    tm = ROW_TILE
    vmem = 2 * tm * d * 2 + 2 * d * tn * 2 + 2 * tm * tn * 2 + 2 * ROW_CHUNK * tn * 4 + (4 << 20)
    return pl.pallas_call(
        _scaled_matmul_kernel,
        grid=(m // tm, n // tn),
        in_specs=[
            pl.BlockSpec((tm, d), lambda i, j: (i, 0)),
            pl.BlockSpec((None, d, tn), lambda i, j: (idx, 0, j)),
            pl.BlockSpec((1, tn), lambda i, j: (0, j)),
        ],
        out_specs=pl.BlockSpec((tm, tn), lambda i, j: (i, j)),
        out_shape=jax.ShapeDtypeStruct((m, n), BF16),
        compiler_params=_params(vmem, ("parallel", "arbitrary")),
        name="qkv_proj",
    )(h, w_stack, colscale)


def _in_proj_kernel(*refs, normalize):
    if normalize:
        x_ref, g_ref, wa_ref, wg_ref, wu_ref, wv_ref, glu_ref, gu_ref, gv_ref, h_ref = refs

        @pl.when(pl.program_id(1) == 0)
        def _():
            _norm_rows_into(x_ref, g_ref, h_ref)
    else:
        h_ref, wa_ref, wg_ref, wu_ref, wv_ref, glu_ref, gu_ref, gv_ref = refs

    h = h_ref[...]
    glu_ref[...] = (_dot(h, wa_ref[...]) * jax.nn.sigmoid(_dot(h, wg_ref[...]))).astype(glu_ref.dtype)
    gu_ref[...] = _gelu_exact(_dot(h, wu_ref[...])).astype(gu_ref.dtype)
    gv_ref[...] = _gelu_exact(_dot(h, wv_ref[...])).astype(gv_ref.dtype)


def _in_proj(x, g, w_stack, idx, tn=512):
    m, d = x.shape
    sec = w_stack.shape[2] // 4
    nb = sec // tn
    tm = ROW_TILE
    normalize = g is not None
    wspec = lambda s: pl.BlockSpec((None, d, tn), lambda i, j: (idx, 0, s * nb + j))
    ospec = pl.BlockSpec((tm, tn), lambda i, j: (i, j))
    oshape = jax.ShapeDtypeStruct((m, sec), BF16)
    x_spec = pl.BlockSpec((tm, d), lambda i, j: (i, 0))
    if normalize:
        in_specs = [x_spec, pl.BlockSpec((1, d), lambda i, j: (0, 0))]
        operands = (x, g.reshape(1, d))
        scratch = [pltpu.VMEM((tm, d), BF16)]
    else:
        in_specs, operands, scratch = [x_spec], (x,), []
    vmem = (2 * tm * d * x.dtype.itemsize + tm * d * 2 + 4 * 2 * d * tn * 2 + 3 * 2 * tm * tn * 2
            + 4 * tm * tn * 4 + (4 << 20))
    return pl.pallas_call(
        functools.partial(_in_proj_kernel, normalize=normalize),
        grid=(m // tm, nb),
        in_specs=in_specs + [wspec(0), wspec(1), wspec(2), wspec(3)],
        out_specs=[ospec, ospec, ospec],
        out_shape=[oshape, oshape, oshape],
        scratch_shapes=scratch,
        compiler_params=_params(vmem, ("parallel", "arbitrary")),
        name="even_in_proj",
    )(*operands, w_stack, w_stack, w_stack, w_stack)


def _conv_kernel(main_ref, halo_ref, w_ref, b_ref, lg_ref, lb_ref, o_ref, win_ref, sh_ref, c_ref, *,
                 tiles_per_seq):
    tc, ch = main_ref.shape
    first = (pl.program_id(0) % tiles_per_seq) == 0
    halo = halo_ref[...].astype(F32)
    win_ref[0:CONV_HALO, :] = jnp.where(first, 0.0, halo)
    win_ref[CONV_HALO:, :] = main_ref[...].astype(F32)

    base = CONV_HALO - (CONV_WIDTH - 1)
    sh_rows = sh_ref.shape[1]
    rb = 128
    for cb in range(ch // V7X_LANES):
        cs = slice(cb * V7X_LANES, (cb + 1) * V7X_LANES)
        for s in range(1, V7X_SUBLANES):
            sh_ref[s - 1] = win_ref[s:s + sh_rows, cs]

        def rows(r, carry):
            r0 = pl.multiple_of(r * rb, rb)
            acc = jnp.zeros((rb, V7X_LANES), F32) + b_ref[:, cs]
            for k in range(CONV_WIDTH):
                s = (base + k) % V7X_SUBLANES
                rs = pl.ds(r0 + (base + k - s), rb)
                tap = win_ref[rs, cs] if s == 0 else sh_ref[s - 1, rs, :]
                acc = acc + tap * w_ref[k:k + 1, cs]
            c_ref[pl.ds(r0, rb), cs] = acc
            return carry

        lax.fori_loop(0, tc // rb, rows, 0)

    nr = NORM_ROWS

    def norm(r, carry):
        rs = pl.ds(pl.multiple_of(r * nr, nr), nr)
        y = _layer_norm_rows(c_ref[rs, :], lg_ref[...], lb_ref[...])
        o_ref[rs, :] = _silu(y).astype(o_ref.dtype)
        return carry

    lax.fori_loop(0, tc // nr, norm, 0)


CONV_ROWS = 512


def _conv_ln_silu(glu, seq, w, b, lg, lb, casts=()):
    m, ch = glu.shape
    tc = CONV_ROWS
    hb = tc // CONV_HALO
    row = lambda v: v.reshape(1, ch)
    c_ops, c_in, c_out, c_shapes, c_vmem = _cast_specs(casts, m // tc, lambda i: i)
    vmem = 2 * tc * ch * 2 * 2 + (tc + CONV_HALO) * ch * 4 + tc * ch * 4 + c_vmem + (6 << 20)
    outs = pl.pallas_call(
        _carrying(functools.partial(_conv_kernel, tiles_per_seq=seq // tc), 6, 1, len(casts)),
        grid=(m // tc,),
        in_specs=[
            pl.BlockSpec((tc, ch), lambda i: (i, 0)),
            pl.BlockSpec((CONV_HALO, ch), lambda i: (jnp.maximum(i * hb - 1, 0), 0)),
            pl.BlockSpec((CONV_WIDTH, ch), lambda i: (0, 0)),
            pl.BlockSpec((1, ch), lambda i: (0, 0)),
            pl.BlockSpec((1, ch), lambda i: (0, 0)),
            pl.BlockSpec((1, ch), lambda i: (0, 0)),
        ] + c_in,
        out_specs=[pl.BlockSpec((tc, ch), lambda i: (i, 0))] + c_out,
        out_shape=[jax.ShapeDtypeStruct((m, ch), BF16)] + c_shapes,
        scratch_shapes=[
            pltpu.VMEM((tc + CONV_HALO, ch), F32),
            pltpu.VMEM((V7X_SUBLANES - 1, tc + CONV_HALO - V7X_SUBLANES, V7X_LANES), F32),
            pltpu.VMEM((tc, ch), F32),
        ],
        compiler_params=_params(vmem, ("parallel",)),
        name="conv_ln_swish",
    )(glu, glu, w, row(b), row(lg), row(lb), *c_ops)
    (a,), copies = _split_outputs(outs, 1, casts)
    return a, copies


def _gating_kernel(u_ref, v_ref, lg_ref, lb_ref, ws_ref, bs_ref, o_ref, vn_ref):
    tg, ch = u_ref.shape
    gd = ch // GMLP_GROUPS
    nr = NORM_ROWS

    def norm(r, carry):
        rs = pl.ds(pl.multiple_of(r * nr, nr), nr)
        vn_ref[rs, :] = _layer_norm_rows(v_ref[rs, :].astype(F32), lg_ref[...], lb_ref[...]).astype(vn_ref.dtype)
        return carry

    lax.fori_loop(0, tg // nr, norm, 0)

    ii = lax.broadcasted_iota(jnp.int32, (GMLP_CHUNK, GMLP_CHUNK), 0) // CHUNK
    jj = lax.broadcasted_iota(jnp.int32, (GMLP_CHUNK, GMLP_CHUNK), 1) // CHUNK
    keep = jj <= ii
    for g in range(GMLP_GROUPS):
        wg = jnp.where(keep, ws_ref[g], 0.0).astype(BF16)
        bcol = bs_ref[:, g:g + 1]
        cs = slice(g * gd, (g + 1) * gd)
        for c in range(tg // GMLP_CHUNK):
            rs = slice(c * GMLP_CHUNK, (c + 1) * GMLP_CHUNK)
            s = _dot(wg, vn_ref[rs, cs]) + bcol
            o_ref[rs, cs] = (u_ref[rs, cs].astype(F32) * s).astype(o_ref.dtype)


def _gating(gu, gv, lg, lb, w_s, b_s, tg=512):
    m, ch = gu.shape
    row = lambda v: v.reshape(1, ch)
    vmem = 3 * 2 * tg * ch * 2 + tg * ch * 2 + (6 << 20)
    blk = pl.BlockSpec((tg, ch), lambda i: (i, 0))
    return pl.pallas_call(
        _gating_kernel,
        grid=(m // tg,),
        in_specs=[
            blk, blk,
            pl.BlockSpec((1, ch), lambda i: (0, 0)),
            pl.BlockSpec((1, ch), lambda i: (0, 0)),
            pl.BlockSpec(w_s.shape, lambda i: (0, 0, 0)),
            pl.BlockSpec((GMLP_CHUNK, GMLP_GROUPS), lambda i: (0, 0)),
        ],
        out_specs=blk,
        out_shape=jax.ShapeDtypeStruct((m, ch), BF16),
        scratch_shapes=[pltpu.VMEM((tg, ch), BF16)],
        compiler_params=_params(vmem, ("parallel",)),
        name="spatial_gating",
    )(gu, gv, row(lg), row(lb), w_s, b_s.T)


def _proj_norm_kernel(*refs, n_parts, last):
    a_refs = refs[:n_parts]
    w_ref, res_ref, g_ref = refs[n_parts:n_parts + 3]
    out_refs = refs[n_parts + 3:]
    kp = a_refs[0].shape[1]
    acc = res_ref[...]
    for p, a_ref in enumerate(a_refs):
        acc = acc + _dot(a_ref[...], w_ref[p * kp:(p + 1) * kp, :])
    normed = _rms_rows(acc, g_ref[...])
    if last:
        out_refs[0][...] = normed
    else:
        out_refs[0][...] = acc
        out_refs[1][...] = normed.astype(out_refs[1].dtype)


def _proj_norm(parts, w_stack, idx, res, g, tm, last=False, casts=()):
    m, n = res.shape
    kp = parts[0].shape[1]
    n_parts = len(parts)
    k_total = w_stack.shape[1]
    c_ops, c_in, c_out, c_shapes, c_vmem = _cast_specs(casts, m // tm, lambda i: i)
    vmem = (k_total * n * 2 + n_parts * 2 * tm * kp * 2 + 2 * tm * n * 4 + 2 * tm * n * 4 + 2 * tm * n * 2
            + 3 * tm * n * 4 + c_vmem + (4 << 20))
    row_blk = lambda width: pl.BlockSpec((tm, width), lambda i: (i, 0))
    if last:
        out_specs, out_shape = [row_blk(n)], [jax.ShapeDtypeStruct((m, n), F32)]
    else:
        out_specs = [row_blk(n), row_blk(n)]
        out_shape = [jax.ShapeDtypeStruct((m, n), F32), jax.ShapeDtypeStruct((m, n), BF16)]
    n_out = len(out_specs)
    outs = pl.pallas_call(
        _carrying(functools.partial(_proj_norm_kernel, n_parts=n_parts, last=last), n_parts + 3, n_out, len(casts)),
        grid=(m // tm,),
        in_specs=[row_blk(kp) for _ in parts] + [
            pl.BlockSpec((None, k_total, n), lambda i: (idx, 0, 0), pipeline_mode=pl.Buffered(1)),
            row_blk(n),
            pl.BlockSpec((1, n), lambda i: (0, 0)),
        ] + c_in,
        out_specs=out_specs + c_out,
        out_shape=out_shape + c_shapes,
        compiler_params=_params(vmem, ("parallel",)),
        name="proj_residual_norm",
    )(*parts, w_stack, res, g.reshape(1, n), *c_ops)
    return _split_outputs(outs, n_out, casts)


def _gate_up_kernel(h_ref, wg_ref, wu_ref, o_ref):
    for r in range(h_ref.shape[0] // ROW_CHUNK):
        rs = slice(r * ROW_CHUNK, (r + 1) * ROW_CHUNK)
        h = h_ref[rs, :]
        o_ref[rs, :] = (_silu(_dot(h, wg_ref[...])) * _dot(h, wu_ref[...])).astype(o_ref.dtype)


def _gate_up(h, w_stack, idx, tf=512):
    m, d = h.shape
    hidden = w_stack.shape[2] // 2
    nf = hidden // tf
    tm = 2 * ROW_TILE
    vmem = 2 * tm * d * 2 + 2 * 2 * d * tf * 2 + 2 * tm * tf * 2 + 6 * ROW_CHUNK * tf * 4 + (4 << 20)
    return pl.pallas_call(
        _gate_up_kernel,
        grid=(m // tm, nf),
        in_specs=[
            pl.BlockSpec((tm, d), lambda i, f: (i, 0)),
            pl.BlockSpec((None, d, tf), lambda i, f: (idx, 0, f)),
            pl.BlockSpec((None, d, tf), lambda i, f: (idx, 0, nf + f)),
        ],
        out_specs=pl.BlockSpec((tm, tf), lambda i, f: (i, f)),
        out_shape=jax.ShapeDtypeStruct((m, hidden), BF16),
        compiler_params=_params(vmem, ("parallel", "arbitrary")),
        name="swiglu_gate_up",
    )(h, w_stack, w_stack)


def _diff_attn_kernel(q_ref, k_ref, v_ref, lq1_ref, lk1_ref, lq2_ref, lk2_ref, sg_ref, o_ref,
                      acc_ref, m_ref, l_ref, s_ref, *, lambda_init, tq):
    seq, hd2 = q_ref.shape
    hd = hd2 // 2
    tk = tq
    nq = seq // tq
    reps = (1, hd2 // V7X_LANES)
    lam = (jnp.exp(jnp.sum(lq1_ref[...] * lk1_ref[...], axis=-1, keepdims=True))
           - jnp.exp(jnp.sum(lq2_ref[...] * lk2_ref[...], axis=-1, keepdims=True)) + lambda_init)

    def rows(blk):
        return pl.ds(pl.multiple_of(blk * tq, tq), tq)

    def reset():
        m_ref[...] = jnp.full(m_ref.shape, -jnp.inf, F32)
        l_ref[...] = jnp.zeros(l_ref.shape, F32)
        acc_ref[...] = jnp.zeros(acc_ref.shape, F32)

    def scores(qi, kb):
        qblk = q_ref[rows(qi), :]
        kblk = k_ref[rows(kb), :]
        return [lax.dot_general(qblk[:, c * hd:(c + 1) * hd], kblk[:, c * hd:(c + 1) * hd],
                                (((1,), (1,)), ((), ())), preferred_element_type=F32) for c in range(2)]

    def consume(kb, masked):
        vblk = v_ref[rows(kb), :]
        if masked:
            qc_id = lax.broadcasted_iota(jnp.int32, (tq, tk), 0) // CHUNK
            kc_id = lax.broadcasted_iota(jnp.int32, (tq, tk), 1) // CHUNK
            keep = kc_id <= qc_id
        for c in range(2):
            s = s_ref[c]
            if masked:
                s = jnp.where(keep, s, -jnp.inf)
            m_prev = m_ref[c]
            m_new = jnp.maximum(m_prev, jnp.max(s, axis=-1, keepdims=True))
            alpha = jnp.exp2(m_prev - m_new)
            p = jnp.exp2(s - jnp.tile(m_new, (1, tk // V7X_LANES)))
            l_ref[c] = alpha * l_ref[c] + jnp.sum(p, axis=-1, keepdims=True)
            acc_ref[c] = jnp.tile(alpha, (1, hd2 // V7X_LANES)) * acc_ref[c] + _dot(p.astype(BF16), vblk)
            m_ref[c] = m_new

    def put_scores(s):
        s_ref[0] = s[0]
        s_ref[1] = s[1]

    def finalize(qi):
        o = acc_ref[0] / jnp.tile(l_ref[0], reps) - lam * (acc_ref[1] / jnp.tile(l_ref[1], reps))
        o = _rms_rows(o, sg_ref[...]) * (1.0 - lambda_init)
        o_ref[rows(qi), :] = o.astype(o_ref.dtype)

    reset()
    put_scores(scores(0, 0))

    def q_tile(qi, carry):
        def full_block(kb, c):
            s_next = scores(qi, kb + 1)
            consume(kb, False)
            put_scores(s_next)
            return c

        lax.fori_loop(0, qi, full_block, 0)
        s_next = scores(jnp.minimum(qi + 1, nq - 1), 0)
        consume(qi, True)
        finalize(qi)
        reset()
        put_scores(s_next)
        return carry

    lax.fori_loop(0, nq, q_tile, 0)


def _diff_attn(qkv, batch, seq, lq1, lk1, lq2, lk2, subln_g, lambda_init, tq=512):
    m = qkv.shape[0]
    hw = qkv.shape[1] // (3 * DIFF_HEADS)
    row = lambda v: v.reshape(1, -1)
    vmem = 4 * 2 * seq * hw * 2 + 2 * tq * hw * 4 + 8 * tq * tq * 4 + (4 << 20)
    lspec = pl.BlockSpec((1, hw // 2), lambda b, h: (0, 0))
    return pl.pallas_call(
        functools.partial(_diff_attn_kernel, lambda_init=lambda_init, tq=tq),
        grid=(batch, DIFF_HEADS),
        in_specs=[
            pl.BlockSpec((seq, hw), lambda b, h: (b, h)),
            pl.BlockSpec((seq, hw), lambda b, h: (b, DIFF_HEADS + h)),
            pl.BlockSpec((seq, hw), lambda b, h: (b, 2 * DIFF_HEADS + h)),
            lspec, lspec, lspec, lspec,
            pl.BlockSpec((1, hw), lambda b, h: (0, 0)),
        ],
        out_specs=pl.BlockSpec((seq, hw), lambda b, h: (b, h)),
        out_shape=jax.ShapeDtypeStruct((m, DIFF_HEADS * hw), BF16),
        scratch_shapes=[
            pltpu.VMEM((2, tq, hw), F32),
            pltpu.VMEM((2, tq, V7X_LANES), F32),
            pltpu.VMEM((2, tq, V7X_LANES), F32),
            pltpu.VMEM((2, tq, tq), F32),
        ],
        compiler_params=_params(vmem, ("parallel", "parallel")),
        name="diff_attention",
    )(qkv, qkv, qkv, row(lq1), row(lk1), row(lq2), row(lk2), row(subln_g))


def kernel(x, w_in_even, conv_w, conv_b, conv_ln_g, conv_ln_b, gmlp_ln_g, gmlp_ln_b, gmlp_w_s, gmlp_b_s,
           w_out_even, w_qkv_odd, w_o_odd, lambda_q1, lambda_k1, lambda_q2, lambda_k2, subln_g, mix_norm_g,
           ffn_norm_g, w_gate_up, w_down, final_norm_g):
    batch, seq, d = x.shape
    depth = mix_norm_g.shape[0]
    m = batch * seq
    head_dim = d // (2 * DIFF_HEADS)
    xf = x.reshape(m, d)

    f32_stacks = {"gate_up": w_gate_up, "down": w_down, "out": w_out_even, "qkv": w_qkv_odd, "o": w_o_odd}
    carrier_steps = {"gate_up": m // CONV_ROWS, "down": m // CONV_ROWS, "out": m // CONV_ROWS,
                     "qkv": m // DOWN_ROWS, "o": m // DOWN_ROWS}
    carried = {k: _can_carry(w, carrier_steps[k]) for k, w in f32_stacks.items()}
    wb = {k: None if carried[k] else w.astype(BF16) for k, w in f32_stacks.items()}
    wb["in"] = w_in_even.astype(BF16)

    def pick(names):
        return [k for k in names if carried[k] and wb[k] is None]

    h = None
    for layer in range(depth):
        i = layer // 2
        if layer % 2 == 0:
            src, gain = (xf, mix_norm_g[layer]) if h is None else (h, None)
            glu, gu, gv = _in_proj(src, gain, wb["in"], i)
            names = pick(["gate_up", "down", "out"])
            a, copies = _conv_ln_silu(glu, seq, conv_w[i], conv_b[i], conv_ln_g[i], conv_ln_b[i],
                                      casts=[f32_stacks[k] for k in names])
            wb.update(zip(names, copies))
            b = _gating(gu, gv, gmlp_ln_g[i], gmlp_ln_b[i], gmlp_w_s[i], gmlp_b_s[i])
            (xf, h), _ = _proj_norm([a, b], wb["out"], i, xf, ffn_norm_g[layer], tm=OUT_ROWS)
        else:
            lambda_init = 0.8 - 0.6 * math.exp(-0.3 * layer)
            colscale = jnp.concatenate([
                jnp.full((1, d), head_dim ** -0.5 * LOG2E, F32), jnp.ones((1, 2 * d), F32)], axis=1)
            qkv = _scaled_matmul(h, wb["qkv"], i, colscale)
            o = _diff_attn(qkv, batch, seq, lambda_q1[i], lambda_k1[i], lambda_q2[i], lambda_k2[i],
                           subln_g[i], lambda_init)
            (xf, h), _ = _proj_norm([o], wb["o"], i, xf, ffn_norm_g[layer], tm=2 * OUT_ROWS)
        act = _gate_up(h, wb["gate_up"], layer)
        if layer == depth - 1:
            (xf,), _ = _proj_norm([act], wb["down"], layer, xf, final_norm_g, tm=DOWN_ROWS, last=True)
        else:
            names = pick(["qkv", "o"])
            (xf, h), copies = _proj_norm([act], wb["down"], layer, xf, mix_norm_g[layer + 1], tm=DOWN_ROWS,
                                         casts=[f32_stacks[k] for k in names])
            wb.update(zip(names, copies))
    return xf.reshape(batch, seq, d)
```

```python
import functools
import math

import jax
import jax.numpy as jnp
from jax import lax
from jax.experimental import pallas as pl
from jax.experimental.pallas import tpu as pltpu

F32 = jnp.float32
BF16 = jnp.bfloat16

EPS = 1e-6
CHUNK = 64
CONV_WIDTH = 31
GMLP_GROUPS = 8
GMLP_CHUNK = 128
DIFF_HEADS = 8
LOG2E = 1.4426950408889634

V7X_LANES = 128
V7X_SUBLANES = 8
V7X_VMEM_LIMIT_BYTES = 54 * 1024 * 1024

ROW_TILE = 1024
ROW_CHUNK = 256
DOWN_ROWS = 256
OUT_ROWS = 256
NORM_ROWS = 128
CONV_HALO = 32


def _dot(a, b):
    return jnp.dot(a, b, preferred_element_type=F32)


def _rms_rows(x, g):
    ms = jnp.mean(x * x, axis=-1, keepdims=True)
    return x * lax.rsqrt(ms + EPS) * g


def _layer_norm_rows(x, g, b):
    mu = jnp.mean(x, axis=-1, keepdims=True)
    d = x - mu
    var = jnp.mean(d * d, axis=-1, keepdims=True)
    return d * lax.rsqrt(var + EPS) * g + b


def _silu(x):
    return x * jax.nn.sigmoid(x)


def _gelu_exact(x):
    return 0.5 * x * (1.0 + lax.erf(x * (1.0 / math.sqrt(2.0))))


def _norm_rows_into(x_ref, g_ref, h_ref):
    def body(r, carry):
        rs = pl.ds(pl.multiple_of(r * NORM_ROWS, NORM_ROWS), NORM_ROWS)
        h_ref[rs, :] = _rms_rows(x_ref[rs, :], g_ref[...]).astype(h_ref.dtype)
        return carry

    lax.fori_loop(0, x_ref.shape[0] // NORM_ROWS, body, 0)


def _params(vmem_bytes, semantics):
    return pltpu.CompilerParams(
        dimension_semantics=semantics,
        vmem_limit_bytes=min(int(vmem_bytes), V7X_VMEM_LIMIT_BYTES),
    )


BF16_ROW_TILE = 16


def _can_carry(w, steps):
    rows = w.shape[0] * w.shape[1]
    return rows > 0 and rows % steps == 0 and (rows // steps) % BF16_ROW_TILE == 0


def _carrying(body, n_in, n_out, n_casts):
    def wrapped(*refs):
        srcs = refs[n_in:n_in + n_casts]
        dsts = refs[n_in + n_casts + n_out:n_in + 2 * n_casts + n_out]
        for src, dst in zip(srcs, dsts):
            dst[...] = src[...].astype(dst.dtype)
        body(*refs[:n_in], *refs[n_in + n_casts:n_in + n_casts + n_out], *refs[n_in + 2 * n_casts + n_out:])

    return wrapped


def _cast_specs(casts, steps, step_index):
    flat = [w.reshape(-1, w.shape[-1]) for w in casts]
    specs = [pl.BlockSpec((w.shape[0] // steps, w.shape[1]), lambda *ids: (step_index(*ids), 0)) for w in flat]
    shapes = [jax.ShapeDtypeStruct(w.shape, BF16) for w in flat]
    vmem = sum(2 * (w.shape[0] // steps) * w.shape[1] * (4 + 2) for w in flat)
    return flat, specs, specs, shapes, vmem


def _split_outputs(outs, n_out, casts):
    return list(outs[:n_out]), [o.reshape(w.shape) for o, w in zip(outs[n_out:], casts)]


def _scaled_matmul_kernel(h_ref, w_ref, cs_ref, o_ref):
    for r in range(h_ref.shape[0] // ROW_CHUNK):
        rs = slice(r * ROW_CHUNK, (r + 1) * ROW_CHUNK)
        o_ref[rs, :] = (_dot(h_ref[rs, :], w_ref[...]) * cs_ref[...]).astype(o_ref.dtype)


def _scaled_matmul(h, w_stack, idx, colscale, tn=2048):
    m, d = h.shape
    n = w_stack.shape[2]
    tm = ROW_TILE
    vmem = 2 * tm * d * 2 + 2 * d * tn * 2 + 2 * tm * tn * 2 + 2 * ROW_CHUNK * tn * 4 + (4 << 20)
    return pl.pallas_call(
        _scaled_matmul_kernel,
        grid=(m // tm, n // tn),
        in_specs=[
            pl.BlockSpec((tm, d), lambda i, j: (i, 0)),
            pl.BlockSpec((None, d, tn), lambda i, j: (idx, 0, j)),
            pl.BlockSpec((1, tn), lambda i, j: (0, j)),
        ],
        out_specs=pl.BlockSpec((tm, tn), lambda i, j: (i, j)),
        out_shape=jax.ShapeDtypeStruct((m, n), BF16),
        compiler_params=_params(vmem, ("parallel", "arbitrary")),
        name="qkv_proj",
    )(h, w_stack, colscale)


def _in_proj_kernel(*refs, normalize):
    if normalize:
        x_ref, g_ref, wa_ref, wg_ref, wu_ref, wv_ref, glu_ref, gu_ref, gv_ref, h_ref = refs

        @pl.when(pl.program_id(1) == 0)
        def _():
            _norm_rows_into(x_ref, g_ref, h_ref)
    else:
        h_ref, wa_ref, wg_ref, wu_ref, wv_ref, glu_ref, gu_ref, gv_ref = refs

    h = h_ref[...]
    glu_ref[...] = (_dot(h, wa_ref[...]) * jax.nn.sigmoid(_dot(h, wg_ref[...]))).astype(glu_ref.dtype)
    gu_ref[...] = _gelu_exact(_dot(h, wu_ref[...])).astype(gu_ref.dtype)
    gv_ref[...] = _gelu_exact(_dot(h, wv_ref[...])).astype(gv_ref.dtype)


def _in_proj(x, g, w_stack, idx, tn=512):
    m, d = x.shape
    sec = w_stack.shape[2] // 4
    nb = sec // tn
    tm = ROW_TILE
    normalize = g is not None
    wspec = lambda s: pl.BlockSpec((None, d, tn), lambda i, j: (idx, 0, s * nb + j))
    ospec = pl.BlockSpec((tm, tn), lambda i, j: (i, j))
    oshape = jax.ShapeDtypeStruct((m, sec), BF16)
    x_spec = pl.BlockSpec((tm, d), lambda i, j: (i, 0))
    if normalize:
        in_specs = [x_spec, pl.BlockSpec((1, d), lambda i, j: (0, 0))]
        operands = (x, g.reshape(1, d))
        scratch = [pltpu.VMEM((tm, d), BF16)]
    else:
        in_specs, operands, scratch = [x_spec], (x,), []
    vmem = (2 * tm * d * x.dtype.itemsize + tm * d * 2 + 4 * 2 * d * tn * 2 + 3 * 2 * tm * tn * 2
            + 4 * tm * tn * 4 + (4 << 20))
    return pl.pallas_call(
        functools.partial(_in_proj_kernel, normalize=normalize),
        grid=(m // tm, nb),
        in_specs=in_specs + [wspec(0), wspec(1), wspec(2), wspec(3)],
        out_specs=[ospec, ospec, ospec],
        out_shape=[oshape, oshape, oshape],
        scratch_shapes=scratch,
        compiler_params=_params(vmem, ("parallel", "arbitrary")),
        name="even_in_proj",
    )(*operands, w_stack, w_stack, w_stack, w_stack)


def _conv_kernel(main_ref, halo_ref, w_ref, b_ref, lg_ref, lb_ref, o_ref, win_ref, sh_ref, c_ref, *,
                 tiles_per_seq):
    tc, ch = main_ref.shape
    first = (pl.program_id(0) % tiles_per_seq) == 0
    halo = halo_ref[...].astype(F32)
    win_ref[0:CONV_HALO, :] = jnp.where(first, 0.0, halo)
    win_ref[CONV_HALO:, :] = main_ref[...].astype(F32)

    base = CONV_HALO - (CONV_WIDTH - 1)
    sh_rows = sh_ref.shape[1]
    rb = 128
    for cb in range(ch // V7X_LANES):
        cs = slice(cb * V7X_LANES, (cb + 1) * V7X_LANES)
        for s in range(1, V7X_SUBLANES):
            sh_ref[s - 1] = win_ref[s:s + sh_rows, cs]

        def rows(r, carry):
            r0 = pl.multiple_of(r * rb, rb)
            acc = jnp.zeros((rb, V7X_LANES), F32) + b_ref[:, cs]
            for k in range(CONV_WIDTH):
                s = (base + k) % V7X_SUBLANES
                rs = pl.ds(r0 + (base + k - s), rb)
                tap = win_ref[rs, cs] if s == 0 else sh_ref[s - 1, rs, :]
                acc = acc + tap * w_ref[k:k + 1, cs]
            c_ref[pl.ds(r0, rb), cs] = acc
            return carry

        lax.fori_loop(0, tc // rb, rows, 0)

    nr = NORM_ROWS

    def norm(r, carry):
        rs = pl.ds(pl.multiple_of(r * nr, nr), nr)
        y = _layer_norm_rows(c_ref[rs, :], lg_ref[...], lb_ref[...])
        o_ref[rs, :] = _silu(y).astype(o_ref.dtype)
        return carry

    lax.fori_loop(0, tc // nr, norm, 0)


CONV_ROWS = 512


def _conv_ln_silu(glu, seq, w, b, lg, lb, casts=()):
    m, ch = glu.shape
    tc = CONV_ROWS
    hb = tc // CONV_HALO
    row = lambda v: v.reshape(1, ch)
    c_ops, c_in, c_out, c_shapes, c_vmem = _cast_specs(casts, m // tc, lambda i: i)
    vmem = 2 * tc * ch * 2 * 2 + (tc + CONV_HALO) * ch * 4 + tc * ch * 4 + c_vmem + (6 << 20)
    outs = pl.pallas_call(
        _carrying(functools.partial(_conv_kernel, tiles_per_seq=seq // tc), 6, 1, len(casts)),
        grid=(m // tc,),
        in_specs=[
            pl.BlockSpec((tc, ch), lambda i: (i, 0)),
            pl.BlockSpec((CONV_HALO, ch), lambda i: (jnp.maximum(i * hb - 1, 0), 0)),
            pl.BlockSpec((CONV_WIDTH, ch), lambda i: (0, 0)),
            pl.BlockSpec((1, ch), lambda i: (0, 0)),
            pl.BlockSpec((1, ch), lambda i: (0, 0)),
            pl.BlockSpec((1, ch), lambda i: (0, 0)),
        ] + c_in,
        out_specs=[pl.BlockSpec((tc, ch), lambda i: (i, 0))] + c_out,
        out_shape=[jax.ShapeDtypeStruct((m, ch), BF16)] + c_shapes,
        scratch_shapes=[
            pltpu.VMEM((tc + CONV_HALO, ch), F32),
            pltpu.VMEM((V7X_SUBLANES - 1, tc + CONV_HALO - V7X_SUBLANES, V7X_LANES), F32),
            pltpu.VMEM((tc, ch), F32),
        ],
        compiler_params=_params(vmem, ("parallel",)),
        name="conv_ln_swish",
    )(glu, glu, w, row(b), row(lg), row(lb), *c_ops)
    (a,), copies = _split_outputs(outs, 1, casts)
    return a, copies


def _gating_kernel(u_ref, v_ref, lg_ref, lb_ref, ws_ref, bs_ref, o_ref, vn_ref):
    tg, ch = u_ref.shape
    gd = ch // GMLP_GROUPS
    nr = NORM_ROWS

    def norm(r, carry):
        rs = pl.ds(pl.multiple_of(r * nr, nr), nr)
        vn_ref[rs, :] = _layer_norm_rows(v_ref[rs, :].astype(F32), lg_ref[...], lb_ref[...]).astype(vn_ref.dtype)
        return carry

    lax.fori_loop(0, tg // nr, norm, 0)

    ii = lax.broadcasted_iota(jnp.int32, (GMLP_CHUNK, GMLP_CHUNK), 0) // CHUNK
    jj = lax.broadcasted_iota(jnp.int32, (GMLP_CHUNK, GMLP_CHUNK), 1) // CHUNK
    keep = jj <= ii
    for g in range(GMLP_GROUPS):
        wg = jnp.where(keep, ws_ref[g], 0.0).astype(BF16)
        bcol = bs_ref[:, g:g + 1]
        cs = slice(g * gd, (g + 1) * gd)
        for c in range(tg // GMLP_CHUNK):
            rs = slice(c * GMLP_CHUNK, (c + 1) * GMLP_CHUNK)
            s = _dot(wg, vn_ref[rs, cs]) + bcol
            o_ref[rs, cs] = (u_ref[rs, cs].astype(F32) * s).astype(o_ref.dtype)


def _gating(gu, gv, lg, lb, w_s, b_s, tg=512):
    m, ch = gu.shape
    row = lambda v: v.reshape(1, ch)
    vmem = 3 * 2 * tg * ch * 2 + tg * ch * 2 + (6 << 20)
    blk = pl.BlockSpec((tg, ch), lambda i: (i, 0))
    return pl.pallas_call(
        _gating_kernel,
        grid=(m // tg,),
        in_specs=[
            blk, blk,
            pl.BlockSpec((1, ch), lambda i: (0, 0)),
            pl.BlockSpec((1, ch), lambda i: (0, 0)),
            pl.BlockSpec(w_s.shape, lambda i: (0, 0, 0)),
            pl.BlockSpec((GMLP_CHUNK, GMLP_GROUPS), lambda i: (0, 0)),
        ],
        out_specs=blk,
        out_shape=jax.ShapeDtypeStruct((m, ch), BF16),
        scratch_shapes=[pltpu.VMEM((tg, ch), BF16)],
        compiler_params=_params(vmem, ("parallel",)),
        name="spatial_gating",
    )(gu, gv, row(lg), row(lb), w_s, b_s.T)


def _proj_norm_kernel(*refs, n_parts, last):
    a_refs = refs[:n_parts]
    w_ref, res_ref, g_ref = refs[n_parts:n_parts + 3]
    out_refs = refs[n_parts + 3:]
    kp = a_refs[0].shape[1]
    acc = res_ref[...]
    for p, a_ref in enumerate(a_refs):
        acc = acc + _dot(a_ref[...], w_ref[p * kp:(p + 1) * kp, :])
    normed = _rms_rows(acc, g_ref[...])
    if last:
        out_refs[0][...] = normed
    else:
        out_refs[0][...] = acc
        out_refs[1][...] = normed.astype(out_refs[1].dtype)


def _proj_norm(parts, w_stack, idx, res, g, tm, last=False, casts=()):
    m, n = res.shape
    kp = parts[0].shape[1]
    n_parts = len(parts)
    k_total = w_stack.shape[1]
    c_ops, c_in, c_out, c_shapes, c_vmem = _cast_specs(casts, m // tm, lambda i: i)
    vmem = (k_total * n * 2 + n_parts * 2 * tm * kp * 2 + 2 * tm * n * 4 + 2 * tm * n * 4 + 2 * tm * n * 2
            + 3 * tm * n * 4 + c_vmem + (4 << 20))
    row_blk = lambda width: pl.BlockSpec((tm, width), lambda i: (i, 0))
    if last:
        out_specs, out_shape = [row_blk(n)], [jax.ShapeDtypeStruct((m, n), F32)]
    else:
        out_specs = [row_blk(n), row_blk(n)]
        out_shape = [jax.ShapeDtypeStruct((m, n), F32), jax.ShapeDtypeStruct((m, n), BF16)]
    n_out = len(out_specs)
    outs = pl.pallas_call(
        _carrying(functools.partial(_proj_norm_kernel, n_parts=n_parts, last=last), n_parts + 3, n_out, len(casts)),
        grid=(m // tm,),
        in_specs=[row_blk(kp) for _ in parts] + [
            pl.BlockSpec((None, k_total, n), lambda i: (idx, 0, 0), pipeline_mode=pl.Buffered(1)),
            row_blk(n),
            pl.BlockSpec((1, n), lambda i: (0, 0)),
        ] + c_in,
        out_specs=out_specs + c_out,
        out_shape=out_shape + c_shapes,
        compiler_params=_params(vmem, ("parallel",)),
        name="proj_residual_norm",
    )(*parts, w_stack, res, g.reshape(1, n), *c_ops)
    return _split_outputs(outs, n_out, casts)


def _gate_up_kernel(h_ref, wg_ref, wu_ref, o_ref):
    for r in range(h_ref.shape[0] // ROW_CHUNK):
        rs = slice(r * ROW_CHUNK, (r + 1) * ROW_CHUNK)
        h = h_ref[rs, :]
        o_ref[rs, :] = (_silu(_dot(h, wg_ref[...])) * _dot(h, wu_ref[...])).astype(o_ref.dtype)


def _gate_up(h, w_stack, idx, tf=512):
    m, d = h.shape
    hidden = w_stack.shape[2] // 2
    nf = hidden // tf
    tm = 2 * ROW_TILE
    vmem = 2 * tm * d * 2 + 2 * 2 * d * tf * 2 + 2 * tm * tf * 2 + 6 * ROW_CHUNK * tf * 4 + (4 << 20)
    return pl.pallas_call(
        _gate_up_kernel,
        grid=(m // tm, nf),
        in_specs=[
            pl.BlockSpec((tm, d), lambda i, f: (i, 0)),
            pl.BlockSpec((None, d, tf), lambda i, f: (idx, 0, f)),
            pl.BlockSpec((None, d, tf), lambda i, f: (idx, 0, nf + f)),
        ],
        out_specs=pl.BlockSpec((tm, tf), lambda i, f: (i, f)),
        out_shape=jax.ShapeDtypeStruct((m, hidden), BF16),
        compiler_params=_params(vmem, ("parallel", "arbitrary")),
        name="swiglu_gate_up",
    )(h, w_stack, w_stack)


def _diff_attn_kernel(q_ref, k_ref, v_ref, lq1_ref, lk1_ref, lq2_ref, lk2_ref, sg_ref, o_ref,
                      acc_ref, m_ref, l_ref, s_ref, kt_ref, *, lambda_init, tq):
    seq, hd2 = q_ref.shape
    hd = hd2 // 2
    tk = tq
    nq = seq // tq
    reps = (1, hd2 // V7X_LANES)
    lam = (jnp.exp(jnp.sum(lq1_ref[...] * lk1_ref[...], axis=-1, keepdims=True))
           - jnp.exp(jnp.sum(lq2_ref[...] * lk2_ref[...], axis=-1, keepdims=True)) + lambda_init)

    def rows(blk):
        return pl.ds(pl.multiple_of(blk * tq, tq), tq)

    def reset():
        m_ref[...] = jnp.full(m_ref.shape, -jnp.inf, F32)
        l_ref[...] = jnp.zeros(l_ref.shape, F32)
        acc_ref[...] = jnp.zeros(acc_ref.shape, F32)

    for kb in range(nq):
        kt_ref[kb] = k_ref[kb * tk:(kb + 1) * tk, :].T

    def scores(qi, kb):
        qblk = q_ref[rows(qi), :]
        kt = kt_ref[kb]
        return [_dot(qblk[:, c * hd:(c + 1) * hd], kt[c * hd:(c + 1) * hd, :]) for c in range(2)]

    def consume(kb, masked):
        vblk = v_ref[rows(kb), :]
        if masked:
            qc_id = lax.broadcasted_iota(jnp.int32, (tq, tk), 0) // CHUNK
            kc_id = lax.broadcasted_iota(jnp.int32, (tq, tk), 1) // CHUNK
            keep = kc_id <= qc_id
        for c in range(2):
            s = s_ref[c]
            if masked:
                s = jnp.where(keep, s, -jnp.inf)
            m_prev = m_ref[c]
            m_new = jnp.maximum(m_prev, jnp.max(s, axis=-1, keepdims=True))
            alpha = jnp.exp2(m_prev - m_new)
            p = jnp.exp2(s - jnp.tile(m_new, (1, tk // V7X_LANES)))
            l_ref[c] = alpha * l_ref[c] + jnp.sum(p, axis=-1, keepdims=True)
            acc_ref[c] = jnp.tile(alpha, (1, hd2 // V7X_LANES)) * acc_ref[c] + _dot(p.astype(BF16), vblk)
            m_ref[c] = m_new

    def put_scores(s):
        s_ref[0] = s[0]
        s_ref[1] = s[1]

    def finalize(qi):
        o = acc_ref[0] / jnp.tile(l_ref[0], reps) - lam * (acc_ref[1] / jnp.tile(l_ref[1], reps))
        o = _rms_rows(o, sg_ref[...]) * (1.0 - lambda_init)
        o_ref[rows(qi), :] = o.astype(o_ref.dtype)

    reset()
    put_scores(scores(0, 0))

    def q_tile(qi, carry):
        def full_block(kb, c):
            s_next = scores(qi, kb + 1)
            consume(kb, False)
            put_scores(s_next)
            return c

        lax.fori_loop(0, qi, full_block, 0)
        s_next = scores(jnp.minimum(qi + 1, nq - 1), 0)
        consume(qi, True)
        finalize(qi)
        reset()
        put_scores(s_next)
        return carry

    lax.fori_loop(0, nq, q_tile, 0)


def _diff_attn(qkv, batch, seq, lq1, lk1, lq2, lk2, subln_g, lambda_init, tq=512):
    m = qkv.shape[0]
    hw = qkv.shape[1] // (3 * DIFF_HEADS)
    row = lambda v: v.reshape(1, -1)
    vmem = 4 * 2 * seq * hw * 2 + seq * hw * 2 + 2 * tq * hw * 4 + 8 * tq * tq * 4 + (4 << 20)
    lspec = pl.BlockSpec((1, hw // 2), lambda b, h: (0, 0))
    return pl.pallas_call(
        functools.partial(_diff_attn_kernel, lambda_init=lambda_init, tq=tq),
        grid=(batch, DIFF_HEADS),
        in_specs=[
            pl.BlockSpec((seq, hw), lambda b, h: (b, h)),
            pl.BlockSpec((seq, hw), lambda b, h: (b, DIFF_HEADS + h)),
            pl.BlockSpec((seq, hw), lambda b, h: (b, 2 * DIFF_HEADS + h)),
            lspec, lspec, lspec, lspec,
            pl.BlockSpec((1, hw), lambda b, h: (0, 0)),
        ],
        out_specs=pl.BlockSpec((seq, hw), lambda b, h: (b, h)),
        out_shape=jax.ShapeDtypeStruct((m, DIFF_HEADS * hw), BF16),
        scratch_shapes=[
            pltpu.VMEM((2, tq, hw), F32),
            pltpu.VMEM((2, tq, V7X_LANES), F32),
            pltpu.VMEM((2, tq, V7X_LANES), F32),
            pltpu.VMEM((2, tq, tq), F32),
            pltpu.VMEM((seq // tq, hw, tq), BF16),
        ],
        compiler_params=_params(vmem, ("parallel", "parallel")),
        name="diff_attention",
    )(qkv, qkv, qkv, row(lq1), row(lk1), row(lq2), row(lk2), row(subln_g))


def kernel(x, w_in_even, conv_w, conv_b, conv_ln_g, conv_ln_b, gmlp_ln_g, gmlp_ln_b, gmlp_w_s, gmlp_b_s,
           w_out_even, w_qkv_odd, w_o_odd, lambda_q1, lambda_k1, lambda_q2, lambda_k2, subln_g, mix_norm_g,
           ffn_norm_g, w_gate_up, w_down, final_norm_g):
    batch, seq, d = x.shape
    depth = mix_norm_g.shape[0]
    m = batch * seq
    head_dim = d // (2 * DIFF_HEADS)
    xf = x.reshape(m, d)

    f32_stacks = {"gate_up": w_gate_up, "down": w_down, "out": w_out_even, "qkv": w_qkv_odd, "o": w_o_odd}
    carrier_steps = {"gate_up": m // CONV_ROWS, "down": m // CONV_ROWS, "out": m // CONV_ROWS,
                     "qkv": m // DOWN_ROWS, "o": m // DOWN_ROWS}
    carried = {k: _can_carry(w, carrier_steps[k]) for k, w in f32_stacks.items()}
    wb = {k: None if carried[k] else w.astype(BF16) for k, w in f32_stacks.items()}
    wb["in"] = w_in_even.astype(BF16)

    def pick(names):
        return [k for k in names if carried[k] and wb[k] is None]

    h = None
    for layer in range(depth):
        i = layer // 2
        if layer % 2 == 0:
            src, gain = (xf, mix_norm_g[layer]) if h is None else (h, None)
            glu, gu, gv = _in_proj(src, gain, wb["in"], i)
            names = pick(["gate_up", "down", "out"])
            a, copies = _conv_ln_silu(glu, seq, conv_w[i], conv_b[i], conv_ln_g[i], conv_ln_b[i],
                                      casts=[f32_stacks[k] for k in names])
            wb.update(zip(names, copies))
            b = _gating(gu, gv, gmlp_ln_g[i], gmlp_ln_b[i], gmlp_w_s[i], gmlp_b_s[i])
            (xf, h), _ = _proj_norm([a, b], wb["out"], i, xf, ffn_norm_g[layer], tm=OUT_ROWS)
        else:
            lambda_init = 0.8 - 0.6 * math.exp(-0.3 * layer)
            colscale = jnp.concatenate([
                jnp.full((1, d), head_dim ** -0.5 * LOG2E, F32), jnp.ones((1, 2 * d), F32)], axis=1)
            qkv = _scaled_matmul(h, wb["qkv"], i, colscale)
            o = _diff_attn(qkv, batch, seq, lambda_q1[i], lambda_k1[i], lambda_q2[i], lambda_k2[i],
                           subln_g[i], lambda_init)
            (xf, h), _ = _proj_norm([o], wb["o"], i, xf, ffn_norm_g[layer], tm=2 * OUT_ROWS)
        act = _gate_up(h, wb["gate_up"], layer)
        if layer == depth - 1:
            (xf,), _ = _proj_norm([act], wb["down"], layer, xf, final_norm_g, tm=DOWN_ROWS, last=True)
        else:
            names = pick(["qkv", "o"])
            (xf, h), copies = _proj_norm([act], wb["down"], layer, xf, mix_norm_g[layer + 1], tm=DOWN_ROWS,
                                         casts=[f32_stacks[k] for k in names])
            wb.update(zip(names, copies))
    return xf.reshape(batch, seq, d)
```

```python
import functools
import math

import jax
import jax.numpy as jnp
from jax import lax
from jax.experimental import pallas as pl
from jax.experimental.pallas import tpu as pltpu

F32 = jnp.float32
BF16 = jnp.bfloat16

EPS = 1e-6
CHUNK = 64
CONV_WIDTH = 31
GMLP_GROUPS = 8
GMLP_CHUNK = 128
DIFF_HEADS = 8
LOG2E = 1.4426950408889634

V7X_LANES = 128
V7X_SUBLANES = 8
V7X_VMEM_LIMIT_BYTES = 54 * 1024 * 1024

ROW_TILE = 1024
ROW_CHUNK = 256
DOWN_ROWS = 256
OUT_ROWS = 512
NORM_ROWS = 128
CONV_HALO = 32


def _dot(a, b):
    return jnp.dot(a, b, preferred_element_type=F32)


def _rms_rows(x, g):
    ms = jnp.mean(x * x, axis=-1, keepdims=True)
    return x * lax.rsqrt(ms + EPS) * g


def _layer_norm_rows(x, g, b):
    mu = jnp.mean(x, axis=-1, keepdims=True)
    d = x - mu
    var = jnp.mean(d * d, axis=-1, keepdims=True)
    return d * lax.rsqrt(var + EPS) * g + b


def _silu(x):
    return x * jax.nn.sigmoid(x)


def _gelu_exact(x):
    return 0.5 * x * (1.0 + lax.erf(x * (1.0 / math.sqrt(2.0))))


def _norm_rows_into(x_ref, g_ref, h_ref):
    def body(r, carry):
        rs = pl.ds(pl.multiple_of(r * NORM_ROWS, NORM_ROWS), NORM_ROWS)
        h_ref[rs, :] = _rms_rows(x_ref[rs, :], g_ref[...]).astype(h_ref.dtype)
        return carry

    lax.fori_loop(0, x_ref.shape[0] // NORM_ROWS, body, 0)


def _params(vmem_bytes, semantics):
    return pltpu.CompilerParams(
        dimension_semantics=semantics,
        vmem_limit_bytes=min(int(vmem_bytes), V7X_VMEM_LIMIT_BYTES),
    )


BF16_ROW_TILE = 16


def _can_carry(w, steps):
    rows = w.shape[0] * w.shape[1]
    return rows > 0 and rows % steps == 0 and (rows // steps) % BF16_ROW_TILE == 0


def _carrying(body, n_in, n_out, n_casts):
    def wrapped(*refs):
        srcs = refs[n_in:n_in + n_casts]
        dsts = refs[n_in + n_casts + n_out:n_in + 2 * n_casts + n_out]
        for src, dst in zip(srcs, dsts):
            dst[...] = src[...].astype(dst.dtype)
        body(*refs[:n_in], *refs[n_in + n_casts:n_in + n_casts + n_out], *refs[n_in + 2 * n_casts + n_out:])

    return wrapped


def _cast_specs(casts, steps, step_index):
    flat = [w.reshape(-1, w.shape[-1]) for w in casts]
    specs = [pl.BlockSpec((w.shape[0] // steps, w.shape[1]), lambda *ids: (step_index(*ids), 0)) for w in flat]
    shapes = [jax.ShapeDtypeStruct(w.shape, BF16) for w in flat]
    vmem = sum(2 * (w.shape[0] // steps) * w.shape[1] * (4 + 2) for w in flat)
    return flat, specs, specs, shapes, vmem


def _split_outputs(outs, n_out, casts):
    return list(outs[:n_out]), [o.reshape(w.shape) for o, w in zip(outs[n_out:], casts)]


def _scaled_matmul_kernel(h_ref, w_ref, cs_ref, o_ref):
    for r in range(h_ref.shape[0] // ROW_CHUNK):
        rs = slice(r * ROW_CHUNK, (r + 1) * ROW_CHUNK)
        o_ref[rs, :] = (_dot(h_ref[rs, :], w_ref[...]) * cs_ref[...]).astype(o_ref.dtype)


def _scaled_matmul(h, w_stack, idx, colscale, tn=2048):
    m, d = h.shape
    n = w_stack.shape[2]
    tm = ROW_TILE
    vmem = 2 * tm * d * 2 + 2 * d * tn * 2 + 2 * tm * tn * 2 + 2 * ROW_CHUNK * tn * 4 + (4 << 20)
    return pl.pallas_call(
        _scaled_matmul_kernel,
        grid=(m // tm, n // tn),
        in_specs=[
            pl.BlockSpec((tm, d), lambda i, j: (i, 0)),
            pl.BlockSpec((None, d, tn), lambda i, j: (idx, 0, j)),
            pl.BlockSpec((1, tn), lambda i, j: (0, j)),
        ],
        out_specs=pl.BlockSpec((tm, tn), lambda i, j: (i, j)),
        out_shape=jax.ShapeDtypeStruct((m, n), BF16),
        compiler_params=_params(vmem, ("parallel", "arbitrary")),
        name="qkv_proj",
    )(h, w_stack, colscale)


def _in_proj_kernel(*refs, normalize):
    if normalize:
        x_ref, g_ref, wa_ref, wg_ref, wu_ref, wv_ref, glu_ref, gu_ref, gv_ref, h_ref = refs

        @pl.when(pl.program_id(1) == 0)
        def _():
            _norm_rows_into(x_ref, g_ref, h_ref)
    else:
        h_ref, wa_ref, wg_ref, wu_ref, wv_ref, glu_ref, gu_ref, gv_ref = refs

    h = h_ref[...]
    glu_ref[...] = (_dot(h, wa_ref[...]) * jax.nn.sigmoid(_dot(h, wg_ref[...]))).astype(glu_ref.dtype)
    gu_ref[...] = _gelu_exact(_dot(h, wu_ref[...])).astype(gu_ref.dtype)
    gv_ref[...] = _gelu_exact(_dot(h, wv_ref[...])).astype(gv_ref.dtype)


def _in_proj(x, g, w_stack, idx, tn=512):
    m, d = x.shape
    sec = w_stack.shape[2] // 4
    nb = sec // tn
    tm = ROW_TILE
    normalize = g is not None
    wspec = lambda s: pl.BlockSpec((None, d, tn), lambda i, j: (idx, 0, s * nb + j))
    ospec = pl.BlockSpec((tm, tn), lambda i, j: (i, j))
    oshape = jax.ShapeDtypeStruct((m, sec), BF16)
    x_spec = pl.BlockSpec((tm, d), lambda i, j: (i, 0))
    if normalize:
        in_specs = [x_spec, pl.BlockSpec((1, d), lambda i, j: (0, 0))]
        operands = (x, g.reshape(1, d))
        scratch = [pltpu.VMEM((tm, d), BF16)]
    else:
        in_specs, operands, scratch = [x_spec], (x,), []
    vmem = (2 * tm * d * x.dtype.itemsize + tm * d * 2 + 4 * 2 * d * tn * 2 + 3 * 2 * tm * tn * 2
            + 4 * tm * tn * 4 + (4 << 20))
    return pl.pallas_call(
        functools.partial(_in_proj_kernel, normalize=normalize),
        grid=(m // tm, nb),
        in_specs=in_specs + [wspec(0), wspec(1), wspec(2), wspec(3)],
        out_specs=[ospec, ospec, ospec],
        out_shape=[oshape, oshape, oshape],
        scratch_shapes=scratch,
        compiler_params=_params(vmem, ("parallel", "arbitrary")),
        name="even_in_proj",
    )(*operands, w_stack, w_stack, w_stack, w_stack)


def _conv_kernel(main_ref, halo_ref, w_ref, b_ref, lg_ref, lb_ref, o_ref, win_ref, sh_ref, c_ref, *,
                 tiles_per_seq):
    tc, ch = main_ref.shape
    first = (pl.program_id(0) % tiles_per_seq) == 0
    halo = halo_ref[...].astype(F32)
    win_ref[0:CONV_HALO, :] = jnp.where(first, 0.0, halo)
    win_ref[CONV_HALO:, :] = main_ref[...].astype(F32)

    base = CONV_HALO - (CONV_WIDTH - 1)
    sh_rows = sh_ref.shape[1]
    rb = 256
    for cb in range(ch // V7X_LANES):
        cs = slice(cb * V7X_LANES, (cb + 1) * V7X_LANES)
        for s in range(1, V7X_SUBLANES):
            sh_ref[s - 1] = win_ref[s:s + sh_rows, cs]

        def rows(r, carry):
            r0 = pl.multiple_of(r * rb, rb)
            acc = jnp.zeros((rb, V7X_LANES), F32) + b_ref[:, cs]
            for k in range(CONV_WIDTH):
                s = (base + k) % V7X_SUBLANES
                rs = pl.ds(r0 + (base + k - s), rb)
                tap = win_ref[rs, cs] if s == 0 else sh_ref[s - 1, rs, :]
                acc = acc + tap * w_ref[k:k + 1, cs]
            c_ref[pl.ds(r0, rb), cs] = acc
            return carry

        lax.fori_loop(0, tc // rb, rows, 0)

    nr = NORM_ROWS

    def norm(r, carry):
        rs = pl.ds(pl.multiple_of(r * nr, nr), nr)
        y = _layer_norm_rows(c_ref[rs, :], lg_ref[...], lb_ref[...])
        o_ref[rs, :] = _silu(y).astype(o_ref.dtype)
        return carry

    lax.fori_loop(0, tc // nr, norm, 0)


CONV_ROWS = 512


def _conv_ln_silu(glu, seq, w, b, lg, lb, casts=()):
    m, ch = glu.shape
    tc = CONV_ROWS
    hb = tc // CONV_HALO
    row = lambda v: v.reshape(1, ch)
    c_ops, c_in, c_out, c_shapes, c_vmem = _cast_specs(casts, m // tc, lambda i: i)
    vmem = 2 * tc * ch * 2 * 2 + (tc + CONV_HALO) * ch * 4 + tc * ch * 4 + c_vmem + (6 << 20)
    outs = pl.pallas_call(
        _carrying(functools.partial(_conv_kernel, tiles_per_seq=seq // tc), 6, 1, len(casts)),
        grid=(m // tc,),
        in_specs=[
            pl.BlockSpec((tc, ch), lambda i: (i, 0)),
            pl.BlockSpec((CONV_HALO, ch), lambda i: (jnp.maximum(i * hb - 1, 0), 0)),
            pl.BlockSpec((CONV_WIDTH, ch), lambda i: (0, 0)),
            pl.BlockSpec((1, ch), lambda i: (0, 0)),
            pl.BlockSpec((1, ch), lambda i: (0, 0)),
            pl.BlockSpec((1, ch), lambda i: (0, 0)),
        ] + c_in,
        out_specs=[pl.BlockSpec((tc, ch), lambda i: (i, 0))] + c_out,
        out_shape=[jax.ShapeDtypeStruct((m, ch), BF16)] + c_shapes,
        scratch_shapes=[
            pltpu.VMEM((tc + CONV_HALO, ch), F32),
            pltpu.VMEM((V7X_SUBLANES - 1, tc + CONV_HALO - V7X_SUBLANES, V7X_LANES), F32),
            pltpu.VMEM((tc, ch), F32),
        ],
        compiler_params=_params(vmem, ("parallel",)),
        name="conv_ln_swish",
    )(glu, glu, w, row(b), row(lg), row(lb), *c_ops)
    (a,), copies = _split_outputs(outs, 1, casts)
    return a, copies


def _gating_kernel(u_ref, v_ref, lg_ref, lb_ref, ws_ref, bs_ref, o_ref, vn_ref):
    tg, ch = u_ref.shape
    gd = ch // GMLP_GROUPS
    nr = NORM_ROWS

    def norm(r, carry):
        rs = pl.ds(pl.multiple_of(r * nr, nr), nr)
        vn_ref[rs, :] = _layer_norm_rows(v_ref[rs, :].astype(F32), lg_ref[...], lb_ref[...]).astype(vn_ref.dtype)
        return carry

    lax.fori_loop(0, tg // nr, norm, 0)

    ii = lax.broadcasted_iota(jnp.int32, (GMLP_CHUNK, GMLP_CHUNK), 0) // CHUNK
    jj = lax.broadcasted_iota(jnp.int32, (GMLP_CHUNK, GMLP_CHUNK), 1) // CHUNK
    keep = jj <= ii
    for g in range(GMLP_GROUPS):
        wg = jnp.where(keep, ws_ref[g], 0.0).astype(BF16)
        bcol = bs_ref[:, g:g + 1]
        cs = slice(g * gd, (g + 1) * gd)
        for c in range(tg // GMLP_CHUNK):
            rs = slice(c * GMLP_CHUNK, (c + 1) * GMLP_CHUNK)
            s = _dot(wg, vn_ref[rs, cs]) + bcol
            o_ref[rs, cs] = (u_ref[rs, cs].astype(F32) * s).astype(o_ref.dtype)


def _gating(gu, gv, lg, lb, w_s, b_s, tg=512):
    m, ch = gu.shape
    row = lambda v: v.reshape(1, ch)
    vmem = 3 * 2 * tg * ch * 2 + tg * ch * 2 + (6 << 20)
    blk = pl.BlockSpec((tg, ch), lambda i: (i, 0))
    return pl.pallas_call(
        _gating_kernel,
        grid=(m // tg,),
        in_specs=[
            blk, blk,
            pl.BlockSpec((1, ch), lambda i: (0, 0)),
            pl.BlockSpec((1, ch), lambda i: (0, 0)),
            pl.BlockSpec(w_s.shape, lambda i: (0, 0, 0)),
            pl.BlockSpec((GMLP_CHUNK, GMLP_GROUPS), lambda i: (0, 0)),
        ],
        out_specs=blk,
        out_shape=jax.ShapeDtypeStruct((m, ch), BF16),
        scratch_shapes=[pltpu.VMEM((tg, ch), BF16)],
        compiler_params=_params(vmem, ("parallel",)),
        name="spatial_gating",
    )(gu, gv, row(lg), row(lb), w_s, b_s.T)


def _proj_norm_kernel(*refs, n_parts, last):
    a_refs = refs[:n_parts]
    w_ref, res_ref, g_ref = refs[n_parts:n_parts + 3]
    out_refs = refs[n_parts + 3:]
    kp = a_refs[0].shape[1]
    acc = res_ref[...]
    for p, a_ref in enumerate(a_refs):
        acc = acc + _dot(a_ref[...], w_ref[p * kp:(p + 1) * kp, :])
    normed = _rms_rows(acc, g_ref[...])
    if last:
        out_refs[0][...] = normed
    else:
        out_refs[0][...] = acc
        out_refs[1][...] = normed.astype(out_refs[1].dtype)


def _proj_norm(parts, w_stack, idx, res, g, tm, last=False, casts=()):
    m, n = res.shape
    kp = parts[0].shape[1]
    n_parts = len(parts)
    k_total = w_stack.shape[1]
    c_ops, c_in, c_out, c_shapes, c_vmem = _cast_specs(casts, m // tm, lambda i: i)
    vmem = (k_total * n * 2 + n_parts * 2 * tm * kp * 2 + 2 * tm * n * 4 + 2 * tm * n * 4 + 2 * tm * n * 2
            + 3 * tm * n * 4 + c_vmem + (4 << 20))
    row_blk = lambda width: pl.BlockSpec((tm, width), lambda i: (i, 0))
    if last:
        out_specs, out_shape = [row_blk(n)], [jax.ShapeDtypeStruct((m, n), F32)]
    else:
        out_specs = [row_blk(n), row_blk(n)]
        out_shape = [jax.ShapeDtypeStruct((m, n), F32), jax.ShapeDtypeStruct((m, n), BF16)]
    n_out = len(out_specs)
    outs = pl.pallas_call(
        _carrying(functools.partial(_proj_norm_kernel, n_parts=n_parts, last=last), n_parts + 3, n_out, len(casts)),
        grid=(m // tm,),
        in_specs=[row_blk(kp) for _ in parts] + [
            pl.BlockSpec((None, k_total, n), lambda i: (idx, 0, 0), pipeline_mode=pl.Buffered(1)),
            row_blk(n),
            pl.BlockSpec((1, n), lambda i: (0, 0)),
        ] + c_in,
        out_specs=out_specs + c_out,
        out_shape=out_shape + c_shapes,
        compiler_params=_params(vmem, ("parallel",)),
        name="proj_residual_norm",
    )(*parts, w_stack, res, g.reshape(1, n), *c_ops)
    return _split_outputs(outs, n_out, casts)


def _gate_up_kernel(h_ref, wg_ref, wu_ref, o_ref):
    for r in range(h_ref.shape[0] // ROW_CHUNK):
        rs = slice(r * ROW_CHUNK, (r + 1) * ROW_CHUNK)
        h = h_ref[rs, :]
        o_ref[rs, :] = (_silu(_dot(h, wg_ref[...])) * _dot(h, wu_ref[...])).astype(o_ref.dtype)


def _gate_up(h, w_stack, idx, tf=512):
    m, d = h.shape
    hidden = w_stack.shape[2] // 2
    nf = hidden // tf
    tm = 4 * ROW_TILE
    vmem = 2 * tm * d * 2 + 2 * 2 * d * tf * 2 + 2 * tm * tf * 2 + 6 * ROW_CHUNK * tf * 4 + (4 << 20)
    return pl.pallas_call(
        _gate_up_kernel,
        grid=(m // tm, nf),
        in_specs=[
            pl.BlockSpec((tm, d), lambda i, f: (i, 0)),
            pl.BlockSpec((None, d, tf), lambda i, f: (idx, 0, f)),
            pl.BlockSpec((None, d, tf), lambda i, f: (idx, 0, nf + f)),
        ],
        out_specs=pl.BlockSpec((tm, tf), lambda i, f: (i, f)),
        out_shape=jax.ShapeDtypeStruct((m, hidden), BF16),
        compiler_params=_params(vmem, ("parallel", "arbitrary")),
        name="swiglu_gate_up",
    )(h, w_stack, w_stack)


def _diff_attn_kernel(q_ref, k_ref, v_ref, lq1_ref, lk1_ref, lq2_ref, lk2_ref, sg_ref, o_ref,
                      acc_ref, m_ref, l_ref, s_ref, kt_ref, *, lambda_init, tq):
    seq, hd2 = q_ref.shape
    hd = hd2 // 2
    tk = tq
    nq = seq // tq
    reps = (1, hd2 // V7X_LANES)
    lam = (jnp.exp(jnp.sum(lq1_ref[...] * lk1_ref[...], axis=-1, keepdims=True))
           - jnp.exp(jnp.sum(lq2_ref[...] * lk2_ref[...], axis=-1, keepdims=True)) + lambda_init)

    def rows(blk):
        return pl.ds(pl.multiple_of(blk * tq, tq), tq)

    def reset():
        m_ref[...] = jnp.full(m_ref.shape, -jnp.inf, F32)
        l_ref[...] = jnp.zeros(l_ref.shape, F32)
        acc_ref[...] = jnp.zeros(acc_ref.shape, F32)

    for kb in range(nq):
        kt_ref[kb] = k_ref[kb * tk:(kb + 1) * tk, :].T

    def scores(qi, kb):
        qblk = q_ref[rows(qi), :]
        kt = kt_ref[kb]
        return [_dot(qblk[:, c * hd:(c + 1) * hd], kt[c * hd:(c + 1) * hd, :]) for c in range(2)]

    def consume(kb, masked):
        vblk = v_ref[rows(kb), :]
        if masked:
            qc_id = lax.broadcasted_iota(jnp.int32, (tq, tk), 0) // CHUNK
            kc_id = lax.broadcasted_iota(jnp.int32, (tq, tk), 1) // CHUNK
            keep = kc_id <= qc_id
        for c in range(2):
            s = s_ref[c]
            if masked:
                s = jnp.where(keep, s, -jnp.inf)
            m_prev = m_ref[c]
            m_new = jnp.maximum(m_prev, jnp.max(s, axis=-1, keepdims=True))
            alpha = jnp.exp2(m_prev - m_new)
            p = jnp.exp2(s - jnp.tile(m_new, (1, tk // V7X_LANES)))
            l_ref[c] = alpha * l_ref[c] + jnp.sum(p, axis=-1, keepdims=True)
            acc_ref[c] = jnp.tile(alpha, (1, hd2 // V7X_LANES)) * acc_ref[c] + _dot(p.astype(BF16), vblk)
            m_ref[c] = m_new

    def put_scores(s):
        s_ref[0] = s[0]
        s_ref[1] = s[1]

    def finalize(qi):
        o = acc_ref[0] / jnp.tile(l_ref[0], reps) - lam * (acc_ref[1] / jnp.tile(l_ref[1], reps))
        o = _rms_rows(o, sg_ref[...]) * (1.0 - lambda_init)
        o_ref[rows(qi), :] = o.astype(o_ref.dtype)

    reset()
    put_scores(scores(0, 0))

    def q_tile(qi, carry):
        def full_block(kb, c):
            s_next = scores(qi, kb + 1)
            consume(kb, False)
            put_scores(s_next)
            return c

        lax.fori_loop(0, qi, full_block, 0)
        s_next = scores(jnp.minimum(qi + 1, nq - 1), 0)
        consume(qi, True)
        finalize(qi)
        reset()
        put_scores(s_next)
        return carry

    lax.fori_loop(0, nq, q_tile, 0)


def _diff_attn(qkv, batch, seq, lq1, lk1, lq2, lk2, subln_g, lambda_init, tq=512):
    m = qkv.shape[0]
    hw = qkv.shape[1] // (3 * DIFF_HEADS)
    row = lambda v: v.reshape(1, -1)
    vmem = 4 * 2 * seq * hw * 2 + seq * hw * 2 + 2 * tq * hw * 4 + 8 * tq * tq * 4 + (4 << 20)
    lspec = pl.BlockSpec((1, hw // 2), lambda b, h: (0, 0))
    return pl.pallas_call(
        functools.partial(_diff_attn_kernel, lambda_init=lambda_init, tq=tq),
        grid=(batch, DIFF_HEADS),
        in_specs=[
            pl.BlockSpec((seq, hw), lambda b, h: (b, h)),
            pl.BlockSpec((seq, hw), lambda b, h: (b, DIFF_HEADS + h)),
            pl.BlockSpec((seq, hw), lambda b, h: (b, 2 * DIFF_HEADS + h)),
            lspec, lspec, lspec, lspec,
            pl.BlockSpec((1, hw), lambda b, h: (0, 0)),
        ],
        out_specs=pl.BlockSpec((seq, hw), lambda b, h: (b, h)),
        out_shape=jax.ShapeDtypeStruct((m, DIFF_HEADS * hw), BF16),
        scratch_shapes=[
            pltpu.VMEM((2, tq, hw), F32),
            pltpu.VMEM((2, tq, V7X_LANES), F32),
            pltpu.VMEM((2, tq, V7X_LANES), F32),
            pltpu.VMEM((2, tq, tq), F32),
            pltpu.VMEM((seq // tq, hw, tq), BF16),
        ],
        compiler_params=_params(vmem, ("parallel", "parallel")),
        name="diff_attention",
    )(qkv, qkv, qkv, row(lq1), row(lk1), row(lq2), row(lk2), row(subln_g))


def kernel(x, w_in_even, conv_w, conv_b, conv_ln_g, conv_ln_b, gmlp_ln_g, gmlp_ln_b, gmlp_w_s, gmlp_b_s,
           w_out_even, w_qkv_odd, w_o_odd, lambda_q1, lambda_k1, lambda_q2, lambda_k2, subln_g, mix_norm_g,
           ffn_norm_g, w_gate_up, w_down, final_norm_g):
    batch, seq, d = x.shape
    depth = mix_norm_g.shape[0]
    m = batch * seq
    head_dim = d // (2 * DIFF_HEADS)
    xf = x.reshape(m, d)

    f32_stacks = {"gate_up": w_gate_up, "down": w_down, "out": w_out_even, "qkv": w_qkv_odd, "o": w_o_odd}
    carrier_steps = {"gate_up": m // CONV_ROWS, "down": m // CONV_ROWS, "out": m // CONV_ROWS,
                     "qkv": m // DOWN_ROWS, "o": m // DOWN_ROWS}
    carried = {k: _can_carry(w, carrier_steps[k]) for k, w in f32_stacks.items()}
    wb = {k: None if carried[k] else w.astype(BF16) for k, w in f32_stacks.items()}
    wb["in"] = w_in_even.astype(BF16)

    def pick(names):
        return [k for k in names if carried[k] and wb[k] is None]

    h = None
    for layer in range(depth):
        i = layer // 2
        if layer % 2 == 0:
            src, gain = (xf, mix_norm_g[layer]) if h is None else (h, None)
            glu, gu, gv = _in_proj(src, gain, wb["in"], i)
            names = pick(["gate_up", "down", "out"])
            a, copies = _conv_ln_silu(glu, seq, conv_w[i], conv_b[i], conv_ln_g[i], conv_ln_b[i],
                                      casts=[f32_stacks[k] for k in names])
            wb.update(zip(names, copies))
            b = _gating(gu, gv, gmlp_ln_g[i], gmlp_ln_b[i], gmlp_w_s[i], gmlp_b_s[i])
            (xf, h), _ = _proj_norm([a, b], wb["out"], i, xf, ffn_norm_g[layer], tm=OUT_ROWS)
        else:
            lambda_init = 0.8 - 0.6 * math.exp(-0.3 * layer)
            colscale = jnp.concatenate([
                jnp.full((1, d), head_dim ** -0.5 * LOG2E, F32), jnp.ones((1, 2 * d), F32)], axis=1)
            qkv = _scaled_matmul(h, wb["qkv"], i, colscale)
            o = _diff_attn(qkv, batch, seq, lambda_q1[i], lambda_k1[i], lambda_q2[i], lambda_k2[i],
                           subln_g[i], lambda_init)
            (xf, h), _ = _proj_norm([o], wb["o"], i, xf, ffn_norm_g[layer], tm=OUT_ROWS)
        act = _gate_up(h, wb["gate_up"], layer)
        if layer == depth - 1:
            (xf,), _ = _proj_norm([act], wb["down"], layer, xf, final_norm_g, tm=DOWN_ROWS, last=True)
        else:
            names = pick(["qkv", "o"])
            (xf, h), copies = _proj_norm([act], wb["down"], layer, xf, mix_norm_g[layer + 1], tm=DOWN_ROWS,
                                         casts=[f32_stacks[k] for k in names])
            wb.update(zip(names, copies))
    return xf.reshape(batch, seq, d)
```

```python
import functools
import math

import jax
import jax.numpy as jnp
from jax import lax
from jax.experimental import pallas as pl
from jax.experimental.pallas import tpu as pltpu

F32 = jnp.float32
BF16 = jnp.bfloat16

EPS = 1e-6
CHUNK = 64
CONV_WIDTH = 31
GMLP_GROUPS = 8
GMLP_CHUNK = 128
DIFF_HEADS = 8
LOG2E = 1.4426950408889634

V7X_LANES = 128
V7X_SUBLANES = 8
V7X_VMEM_LIMIT_BYTES = 54 * 1024 * 1024

ROW_TILE = 1024
ROW_CHUNK = 256
DOWN_ROWS = 256
OUT_ROWS = 512
NORM_ROWS = 256
CONV_HALO = 32


def _dot(a, b):
    return jnp.dot(a, b, preferred_element_type=F32)


def _rms_rows(x, g):
    ms = jnp.mean(x * x, axis=-1, keepdims=True)
    return x * lax.rsqrt(ms + EPS) * g


def _layer_norm_rows(x, g, b):
    mu = jnp.mean(x, axis=-1, keepdims=True)
    d = x - mu
    var = jnp.mean(d * d, axis=-1, keepdims=True)
    return d * lax.rsqrt(var + EPS) * g + b


def _silu(x):
    return x * jax.nn.sigmoid(x)


def _gelu_exact(x):
    return 0.5 * x * (1.0 + lax.erf(x * (1.0 / math.sqrt(2.0))))


def _norm_rows_into(x_ref, g_ref, h_ref):
    def body(r, carry):
        rs = pl.ds(pl.multiple_of(r * NORM_ROWS, NORM_ROWS), NORM_ROWS)
        h_ref[rs, :] = _rms_rows(x_ref[rs, :], g_ref[...]).astype(h_ref.dtype)
        return carry

    lax.fori_loop(0, x_ref.shape[0] // NORM_ROWS, body, 0)


def _params(vmem_bytes, semantics):
    return pltpu.CompilerParams(
        dimension_semantics=semantics,
        vmem_limit_bytes=min(int(vmem_bytes), V7X_VMEM_LIMIT_BYTES),
    )


BF16_ROW_TILE = 16


def _can_carry(w, steps):
    rows = w.shape[0] * w.shape[1]
    return rows > 0 and rows % steps == 0 and (rows // steps) % BF16_ROW_TILE == 0


def _carrying(body, n_in, n_out, n_casts):
    def wrapped(*refs):
        srcs = refs[n_in:n_in + n_casts]
        dsts = refs[n_in + n_casts + n_out:n_in + 2 * n_casts + n_out]
        for src, dst in zip(srcs, dsts):
            dst[...] = src[...].astype(dst.dtype)
        body(*refs[:n_in], *refs[n_in + n_casts:n_in + n_casts + n_out], *refs[n_in + 2 * n_casts + n_out:])

    return wrapped


def _cast_specs(casts, steps, step_index):
    flat = [w.reshape(-1, w.shape[-1]) for w in casts]
    specs = [pl.BlockSpec((w.shape[0] // steps, w.shape[1]), lambda *ids: (step_index(*ids), 0)) for w in flat]
    shapes = [jax.ShapeDtypeStruct(w.shape, BF16) for w in flat]
    vmem = sum(2 * (w.shape[0] // steps) * w.shape[1] * (4 + 2) for w in flat)
    return flat, specs, specs, shapes, vmem


def _split_outputs(outs, n_out, casts):
    return list(outs[:n_out]), [o.reshape(w.shape) for o, w in zip(outs[n_out:], casts)]


def _scaled_matmul_kernel(h_ref, w_ref, cs_ref, o_ref):
    for r in range(h_ref.shape[0] // ROW_CHUNK):
        rs = slice(r * ROW_CHUNK, (r + 1) * ROW_CHUNK)
        o_ref[rs, :] = (_dot(h_ref[rs, :], w_ref[...]) * cs_ref[...]).astype(o_ref.dtype)


def _scaled_matmul(h, w_stack, idx, colscale, tn=2048):
    m, d = h.shape
    n = w_stack.shape[2]
    tm = ROW_TILE
    vmem = 2 * tm * d * 2 + 2 * d * tn * 2 + 2 * tm * tn * 2 + 2 * ROW_CHUNK * tn * 4 + (4 << 20)
    return pl.pallas_call(
        _scaled_matmul_kernel,
        grid=(m // tm, n // tn),
        in_specs=[
            pl.BlockSpec((tm, d), lambda i, j: (i, 0)),
            pl.BlockSpec((None, d, tn), lambda i, j: (idx, 0, j)),
            pl.BlockSpec((1, tn), lambda i, j: (0, j)),
        ],
        out_specs=pl.BlockSpec((tm, tn), lambda i, j: (i, j)),
        out_shape=jax.ShapeDtypeStruct((m, n), BF16),
        compiler_params=_params(vmem, ("parallel", "arbitrary")),
        name="qkv_proj",
    )(h, w_stack, colscale)


def _in_proj_kernel(*refs, normalize):
    if normalize:
        x_ref, g_ref, wa_ref, wg_ref, wu_ref, wv_ref, glu_ref, gu_ref, gv_ref, h_ref = refs

        @pl.when(pl.program_id(1) == 0)
        def _():
            _norm_rows_into(x_ref, g_ref, h_ref)
    else:
        h_ref, wa_ref, wg_ref, wu_ref, wv_ref, glu_ref, gu_ref, gv_ref = refs

    h = h_ref[...]
    glu_ref[...] = (_dot(h, wa_ref[...]) * jax.nn.sigmoid(_dot(h, wg_ref[...]))).astype(glu_ref.dtype)
    gu_ref[...] = _gelu_exact(_dot(h, wu_ref[...])).astype(gu_ref.dtype)
    gv_ref[...] = _gelu_exact(_dot(h, wv_ref[...])).astype(gv_ref.dtype)


def _in_proj(x, g, w_stack, idx, tn=512):
    m, d = x.shape
    sec = w_stack.shape[2] // 4
    nb = sec // tn
    tm = ROW_TILE
    normalize = g is not None
    wspec = lambda s: pl.BlockSpec((None, d, tn), lambda i, j: (idx, 0, s * nb + j))
    ospec = pl.BlockSpec((tm, tn), lambda i, j: (i, j))
    oshape = jax.ShapeDtypeStruct((m, sec), BF16)
    x_spec = pl.BlockSpec((tm, d), lambda i, j: (i, 0))
    if normalize:
        in_specs = [x_spec, pl.BlockSpec((1, d), lambda i, j: (0, 0))]
        operands = (x, g.reshape(1, d))
        scratch = [pltpu.VMEM((tm, d), BF16)]
    else:
        in_specs, operands, scratch = [x_spec], (x,), []
    vmem = (2 * tm * d * x.dtype.itemsize + tm * d * 2 + 4 * 2 * d * tn * 2 + 3 * 2 * tm * tn * 2
            + 4 * tm * tn * 4 + (4 << 20))
    return pl.pallas_call(
        functools.partial(_in_proj_kernel, normalize=normalize),
        grid=(m // tm, nb),
        in_specs=in_specs + [wspec(0), wspec(1), wspec(2), wspec(3)],
        out_specs=[ospec, ospec, ospec],
        out_shape=[oshape, oshape, oshape],
        scratch_shapes=scratch,
        compiler_params=_params(vmem, ("parallel", "arbitrary")),
        name="even_in_proj",
    )(*operands, w_stack, w_stack, w_stack, w_stack)


def _conv_kernel(main_ref, halo_ref, w_ref, b_ref, lg_ref, lb_ref, o_ref, win_ref, sh_ref, c_ref, *,
                 tiles_per_seq):
    tc, ch = main_ref.shape
    first = (pl.program_id(0) % tiles_per_seq) == 0
    halo = halo_ref[...].astype(F32)
    win_ref[0:CONV_HALO, :] = jnp.where(first, 0.0, halo)
    win_ref[CONV_HALO:, :] = main_ref[...].astype(F32)

    base = CONV_HALO - (CONV_WIDTH - 1)
    sh_rows = sh_ref.shape[1]
    rb = 256
    for cb in range(ch // V7X_LANES):
        cs = slice(cb * V7X_LANES, (cb + 1) * V7X_LANES)
        for s in range(1, V7X_SUBLANES):
            sh_ref[s - 1] = win_ref[s:s + sh_rows, cs]

        def rows(r, carry):
            r0 = pl.multiple_of(r * rb, rb)
            acc = jnp.zeros((rb, V7X_LANES), F32) + b_ref[:, cs]
            for k in range(CONV_WIDTH):
                s = (base + k) % V7X_SUBLANES
                rs = pl.ds(r0 + (base + k - s), rb)
                tap = win_ref[rs, cs] if s == 0 else sh_ref[s - 1, rs, :]
                acc = acc + tap * w_ref[k:k + 1, cs]
            c_ref[pl.ds(r0, rb), cs] = acc
            return carry

        lax.fori_loop(0, tc // rb, rows, 0)

    nr = NORM_ROWS

    def norm(r, carry):
        rs = pl.ds(pl.multiple_of(r * nr, nr), nr)
        y = _layer_norm_rows(c_ref[rs, :], lg_ref[...], lb_ref[...])
        o_ref[rs, :] = _silu(y).astype(o_ref.dtype)
        return carry

    lax.fori_loop(0, tc // nr, norm, 0)


CONV_ROWS = 512


def _conv_ln_silu(glu, seq, w, b, lg, lb, casts=()):
    m, ch = glu.shape
    tc = CONV_ROWS
    hb = tc // CONV_HALO
    row = lambda v: v.reshape(1, ch)
    c_ops, c_in, c_out, c_shapes, c_vmem = _cast_specs(casts, m // tc, lambda i: i)
    vmem = 2 * tc * ch * 2 * 2 + (tc + CONV_HALO) * ch * 4 + tc * ch * 4 + c_vmem + (6 << 20)
    outs = pl.pallas_call(
        _carrying(functools.partial(_conv_kernel, tiles_per_seq=seq // tc), 6, 1, len(casts)),
        grid=(m // tc,),
        in_specs=[
            pl.BlockSpec((tc, ch), lambda i: (i, 0)),
            pl.BlockSpec((CONV_HALO, ch), lambda i: (jnp.maximum(i * hb - 1, 0), 0)),
            pl.BlockSpec((CONV_WIDTH, ch), lambda i: (0, 0)),
            pl.BlockSpec((1, ch), lambda i: (0, 0)),
            pl.BlockSpec((1, ch), lambda i: (0, 0)),
            pl.BlockSpec((1, ch), lambda i: (0, 0)),
        ] + c_in,
        out_specs=[pl.BlockSpec((tc, ch), lambda i: (i, 0))] + c_out,
        out_shape=[jax.ShapeDtypeStruct((m, ch), BF16)] + c_shapes,
        scratch_shapes=[
            pltpu.VMEM((tc + CONV_HALO, ch), F32),
            pltpu.VMEM((V7X_SUBLANES - 1, tc + CONV_HALO - V7X_SUBLANES, V7X_LANES), F32),
            pltpu.VMEM((tc, ch), F32),
        ],
        compiler_params=_params(vmem, ("parallel",)),
        name="conv_ln_swish",
    )(glu, glu, w, row(b), row(lg), row(lb), *c_ops)
    (a,), copies = _split_outputs(outs, 1, casts)
    return a, copies


def _gating_kernel(u_ref, v_ref, lg_ref, lb_ref, ws_ref, bs_ref, o_ref, vn_ref):
    tg, ch = u_ref.shape
    gd = ch // GMLP_GROUPS
    nr = NORM_ROWS

    def norm(r, carry):
        rs = pl.ds(pl.multiple_of(r * nr, nr), nr)
        vn_ref[rs, :] = _layer_norm_rows(v_ref[rs, :].astype(F32), lg_ref[...], lb_ref[...]).astype(vn_ref.dtype)
        return carry

    lax.fori_loop(0, tg // nr, norm, 0)

    ii = lax.broadcasted_iota(jnp.int32, (GMLP_CHUNK, GMLP_CHUNK), 0) // CHUNK
    jj = lax.broadcasted_iota(jnp.int32, (GMLP_CHUNK, GMLP_CHUNK), 1) // CHUNK
    keep = jj <= ii
    for g in range(GMLP_GROUPS):
        wg = jnp.where(keep, ws_ref[g], 0.0).astype(BF16)
        bcol = bs_ref[:, g:g + 1]
        cs = slice(g * gd, (g + 1) * gd)
        for c in range(tg // GMLP_CHUNK):
            rs = slice(c * GMLP_CHUNK, (c + 1) * GMLP_CHUNK)
            s = _dot(wg, vn_ref[rs, cs]) + bcol
            o_ref[rs, cs] = (u_ref[rs, cs].astype(F32) * s).astype(o_ref.dtype)


def _gating(gu, gv, lg, lb, w_s, b_s, tg=512):
    m, ch = gu.shape
    row = lambda v: v.reshape(1, ch)
    vmem = 3 * 2 * tg * ch * 2 + tg * ch * 2 + (6 << 20)
    blk = pl.BlockSpec((tg, ch), lambda i: (i, 0))
    return pl.pallas_call(
        _gating_kernel,
        grid=(m // tg,),
        in_specs=[
            blk, blk,
            pl.BlockSpec((1, ch), lambda i: (0, 0)),
            pl.BlockSpec((1, ch), lambda i: (0, 0)),
            pl.BlockSpec(w_s.shape, lambda i: (0, 0, 0)),
            pl.BlockSpec((GMLP_CHUNK, GMLP_GROUPS), lambda i: (0, 0)),
        ],
        out_specs=blk,
        out_shape=jax.ShapeDtypeStruct((m, ch), BF16),
        scratch_shapes=[pltpu.VMEM((tg, ch), BF16)],
        compiler_params=_params(vmem, ("parallel",)),
        name="spatial_gating",
    )(gu, gv, row(lg), row(lb), w_s, b_s.T)


def _proj_norm_kernel(*refs, n_parts, last):
    a_refs = refs[:n_parts]
    w_ref, res_ref, g_ref = refs[n_parts:n_parts + 3]
    out_refs = refs[n_parts + 3:]
    kp = a_refs[0].shape[1]
    acc = res_ref[...]
    for p, a_ref in enumerate(a_refs):
        acc = acc + _dot(a_ref[...], w_ref[p * kp:(p + 1) * kp, :])
    normed = _rms_rows(acc, g_ref[...])
    if last:
        out_refs[0][...] = normed
    else:
        out_refs[0][...] = acc
        out_refs[1][...] = normed.astype(out_refs[1].dtype)


def _proj_norm(parts, w_stack, idx, res, g, tm, last=False, casts=()):
    m, n = res.shape
    kp = parts[0].shape[1]
    n_parts = len(parts)
    k_total = w_stack.shape[1]
    c_ops, c_in, c_out, c_shapes, c_vmem = _cast_specs(casts, m // tm, lambda i: i)
    vmem = (k_total * n * 2 + n_parts * 2 * tm * kp * 2 + 2 * tm * n * 4 + 2 * tm * n * 4 + 2 * tm * n * 2
            + 3 * tm * n * 4 + c_vmem + (4 << 20))
    row_blk = lambda width: pl.BlockSpec((tm, width), lambda i: (i, 0))
    if last:
        out_specs, out_shape = [row_blk(n)], [jax.ShapeDtypeStruct((m, n), F32)]
    else:
        out_specs = [row_blk(n), row_blk(n)]
        out_shape = [jax.ShapeDtypeStruct((m, n), F32), jax.ShapeDtypeStruct((m, n), BF16)]
    n_out = len(out_specs)
    outs = pl.pallas_call(
        _carrying(functools.partial(_proj_norm_kernel, n_parts=n_parts, last=last), n_parts + 3, n_out, len(casts)),
        grid=(m // tm,),
        in_specs=[row_blk(kp) for _ in parts] + [
            pl.BlockSpec((None, k_total, n), lambda i: (idx, 0, 0), pipeline_mode=pl.Buffered(1)),
            row_blk(n),
            pl.BlockSpec((1, n), lambda i: (0, 0)),
        ] + c_in,
        out_specs=out_specs + c_out,
        out_shape=out_shape + c_shapes,
        compiler_params=_params(vmem, ("parallel",)),
        name="proj_residual_norm",
    )(*parts, w_stack, res, g.reshape(1, n), *c_ops)
    return _split_outputs(outs, n_out, casts)


def _gate_up_kernel(h_ref, wg_ref, wu_ref, o_ref):
    for r in range(h_ref.shape[0] // ROW_CHUNK):
        rs = slice(r * ROW_CHUNK, (r + 1) * ROW_CHUNK)
        h = h_ref[rs, :]
        o_ref[rs, :] = (_silu(_dot(h, wg_ref[...])) * _dot(h, wu_ref[...])).astype(o_ref.dtype)


def _gate_up(h, w_stack, idx, tf=512):
    m, d = h.shape
    hidden = w_stack.shape[2] // 2
    nf = hidden // tf
    tm = 4 * ROW_TILE
    vmem = 2 * tm * d * 2 + 2 * 2 * d * tf * 2 + 2 * tm * tf * 2 + 6 * ROW_CHUNK * tf * 4 + (4 << 20)
    return pl.pallas_call(
        _gate_up_kernel,
        grid=(m // tm, nf),
        in_specs=[
            pl.BlockSpec((tm, d), lambda i, f: (i, 0)),
            pl.BlockSpec((None, d, tf), lambda i, f: (idx, 0, f)),
            pl.BlockSpec((None, d, tf), lambda i, f: (idx, 0, nf + f)),
        ],
        out_specs=pl.BlockSpec((tm, tf), lambda i, f: (i, f)),
        out_shape=jax.ShapeDtypeStruct((m, hidden), BF16),
        compiler_params=_params(vmem, ("parallel", "arbitrary")),
        name="swiglu_gate_up",
    )(h, w_stack, w_stack)


def _diff_attn_kernel(q_ref, k_ref, v_ref, lq1_ref, lk1_ref, lq2_ref, lk2_ref, sg_ref, o_ref,
                      acc_ref, m_ref, l_ref, s_ref, kt_ref, *, lambda_init, tq):
    seq, hd2 = q_ref.shape
    hd = hd2 // 2
    tk = tq
    nq = seq // tq
    reps = (1, hd2 // V7X_LANES)
    lam = (jnp.exp(jnp.sum(lq1_ref[...] * lk1_ref[...], axis=-1, keepdims=True))
           - jnp.exp(jnp.sum(lq2_ref[...] * lk2_ref[...], axis=-1, keepdims=True)) + lambda_init)

    def rows(blk):
        return pl.ds(pl.multiple_of(blk * tq, tq), tq)

    def reset():
        m_ref[...] = jnp.full(m_ref.shape, -jnp.inf, F32)
        l_ref[...] = jnp.zeros(l_ref.shape, F32)
        acc_ref[...] = jnp.zeros(acc_ref.shape, F32)

    for kb in range(nq):
        kt_ref[kb] = k_ref[kb * tk:(kb + 1) * tk, :].T

    def scores(qi, kb):
        qblk = q_ref[rows(qi), :]
        kt = kt_ref[kb]
        return [_dot(qblk[:, c * hd:(c + 1) * hd], kt[c * hd:(c + 1) * hd, :]) for c in range(2)]

    def consume(kb, masked):
        vblk = v_ref[rows(kb), :]
        if masked:
            qc_id = lax.broadcasted_iota(jnp.int32, (tq, tk), 0) // CHUNK
            kc_id = lax.broadcasted_iota(jnp.int32, (tq, tk), 1) // CHUNK
            keep = kc_id <= qc_id
        for c in range(2):
            s = s_ref[c]
            if masked:
                s = jnp.where(keep, s, -jnp.inf)
            m_prev = m_ref[c]
            m_new = jnp.maximum(m_prev, jnp.max(s, axis=-1, keepdims=True))
            alpha = jnp.exp2(m_prev - m_new)
            p = jnp.exp2(s - jnp.tile(m_new, (1, tk // V7X_LANES)))
            l_ref[c] = alpha * l_ref[c] + jnp.sum(p, axis=-1, keepdims=True)
            acc_ref[c] = jnp.tile(alpha, (1, hd2 // V7X_LANES)) * acc_ref[c] + _dot(p.astype(BF16), vblk)
            m_ref[c] = m_new

    def put_scores(s):
        s_ref[0] = s[0]
        s_ref[1] = s[1]

    def finalize(qi):
        o = acc_ref[0] / jnp.tile(l_ref[0], reps) - lam * (acc_ref[1] / jnp.tile(l_ref[1], reps))
        o = _rms_rows(o, sg_ref[...]) * (1.0 - lambda_init)
        o_ref[rows(qi), :] = o.astype(o_ref.dtype)

    reset()
    put_scores(scores(0, 0))

    def q_tile(qi, carry):
        def full_block(kb, c):
            s_next = scores(qi, kb + 1)
            consume(kb, False)
            put_scores(s_next)
            return c

        lax.fori_loop(0, qi, full_block, 0)
        s_next = scores(jnp.minimum(qi + 1, nq - 1), 0)
        consume(qi, True)
        finalize(qi)
        reset()
        put_scores(s_next)
        return carry

    lax.fori_loop(0, nq, q_tile, 0)


def _diff_attn(qkv, batch, seq, lq1, lk1, lq2, lk2, subln_g, lambda_init, tq=512):
    m = qkv.shape[0]
    hw = qkv.shape[1] // (3 * DIFF_HEADS)
    row = lambda v: v.reshape(1, -1)
    vmem = 4 * 2 * seq * hw * 2 + seq * hw * 2 + 2 * tq * hw * 4 + 8 * tq * tq * 4 + (4 << 20)
    lspec = pl.BlockSpec((1, hw // 2), lambda b, h: (0, 0))
    return pl.pallas_call(
        functools.partial(_diff_attn_kernel, lambda_init=lambda_init, tq=tq),
        grid=(batch, DIFF_HEADS),
        in_specs=[
            pl.BlockSpec((seq, hw), lambda b, h: (b, h)),
            pl.BlockSpec((seq, hw), lambda b, h: (b, DIFF_HEADS + h)),
            pl.BlockSpec((seq, hw), lambda b, h: (b, 2 * DIFF_HEADS + h)),
            lspec, lspec, lspec, lspec,
            pl.BlockSpec((1, hw), lambda b, h: (0, 0)),
        ],
        out_specs=pl.BlockSpec((seq, hw), lambda b, h: (b, h)),
        out_shape=jax.ShapeDtypeStruct((m, DIFF_HEADS * hw), BF16),
        scratch_shapes=[
            pltpu.VMEM((2, tq, hw), F32),
            pltpu.VMEM((2, tq, V7X_LANES), F32),
            pltpu.VMEM((2, tq, V7X_LANES), F32),
            pltpu.VMEM((2, tq, tq), F32),
            pltpu.VMEM((seq // tq, hw, tq), BF16),
        ],
        compiler_params=_params(vmem, ("parallel", "parallel")),
        name="diff_attention",
    )(qkv, qkv, qkv, row(lq1), row(lk1), row(lq2), row(lk2), row(subln_g))


def kernel(x, w_in_even, conv_w, conv_b, conv_ln_g, conv_ln_b, gmlp_ln_g, gmlp_ln_b, gmlp_w_s, gmlp_b_s,
           w_out_even, w_qkv_odd, w_o_odd, lambda_q1, lambda_k1, lambda_q2, lambda_k2, subln_g, mix_norm_g,
           ffn_norm_g, w_gate_up, w_down, final_norm_g):
    batch, seq, d = x.shape
    depth = mix_norm_g.shape[0]
    m = batch * seq
    head_dim = d // (2 * DIFF_HEADS)
    xf = x.reshape(m, d)

    f32_stacks = {"gate_up": w_gate_up, "down": w_down, "out": w_out_even, "qkv": w_qkv_odd, "o": w_o_odd}
    carrier_steps = {"gate_up": m // CONV_ROWS, "down": m // CONV_ROWS, "out": m // CONV_ROWS,
                     "qkv": m // DOWN_ROWS, "o": m // DOWN_ROWS}
    carried = {k: _can_carry(w, carrier_steps[k]) for k, w in f32_stacks.items()}
    wb = {k: None if carried[k] else w.astype(BF16) for k, w in f32_stacks.items()}
    wb["in"] = w_in_even.astype(BF16)

    def pick(names):
        return [k for k in names if carried[k] and wb[k] is None]

    h = None
    for layer in range(depth):
        i = layer // 2
        if layer % 2 == 0:
            src, gain = (xf, mix_norm_g[layer]) if h is None else (h, None)
            glu, gu, gv = _in_proj(src, gain, wb["in"], i)
            names = pick(["gate_up", "down", "out"])
            a, copies = _conv_ln_silu(glu, seq, conv_w[i], conv_b[i], conv_ln_g[i], conv_ln_b[i],
                                      casts=[f32_stacks[k] for k in names])
            wb.update(zip(names, copies))
            b = _gating(gu, gv, gmlp_ln_g[i], gmlp_ln_b[i], gmlp_w_s[i], gmlp_b_s[i])
            (xf, h), _ = _proj_norm([a, b], wb["out"], i, xf, ffn_norm_g[layer], tm=OUT_ROWS)
        else:
            lambda_init = 0.8 - 0.6 * math.exp(-0.3 * layer)
            colscale = jnp.concatenate([
                jnp.full((1, d), head_dim ** -0.5 * LOG2E, F32), jnp.ones((1, 2 * d), F32)], axis=1)
            qkv = _scaled_matmul(h, wb["qkv"], i, colscale)
            o = _diff_attn(qkv, batch, seq, lambda_q1[i], lambda_k1[i], lambda_q2[i], lambda_k2[i],
                           subln_g[i], lambda_init)
            (xf, h), _ = _proj_norm([o], wb["o"], i, xf, ffn_norm_g[layer], tm=OUT_ROWS)
        act = _gate_up(h, wb["gate_up"], layer)
        if layer == depth - 1:
            (xf,), _ = _proj_norm([act], wb["down"], layer, xf, final_norm_g, tm=DOWN_ROWS, last=True)
        else:
            names = pick(["qkv", "o"])
            (xf, h), copies = _proj_norm([act], wb["down"], layer, xf, mix_norm_g[layer + 1], tm=DOWN_ROWS,
                                         casts=[f32_stacks[k] for k in names])
            wb.update(zip(names, copies))
    return xf.reshape(batch, seq, d)
```
